```python
import jax, jax.numpy as jnp
from jax import lax
import numpy as np

D_MODEL = 2048
BATCH = 1
SEQ = 8192
DEPTH = 4

CTX_LEN = 256
GRID_W = 64

HEAD_DIM = 128
N_HEADS = D_MODEL // HEAD_DIM
N_KV_HEADS = N_HEADS // 4
ATTN_DIM = N_HEADS * HEAD_DIM
KV_DIM = N_KV_HEADS * HEAD_DIM
Q_BLOCK = 128
ROPE_THETA = 10000.0

F_GROUPS = 4
F_DIM = D_MODEL // 2
F_GROUP_DIM = F_DIM // F_GROUPS

CONV_DIM = D_MODEL // 2
CONV_WIDTH = 3

N_BRANCH = 3
EPS = 1e-6

IN_SIZES = (
    ATTN_DIM,
    KV_DIM,
    KV_DIM,
    ATTN_DIM,
    F_DIM,
    F_DIM,
    CONV_DIM,
    CONV_DIM,
    CONV_DIM,
    CONV_DIM,
    N_BRANCH * D_MODEL,
)
IN_COLS = int(sum(IN_SIZES))
IN_OFFSETS = tuple(int(v) for v in np.cumsum(IN_SIZES)[:-1])
KV_START = ATTN_DIM
KV_END = ATTN_DIM + 2 * KV_DIM

kernel_name = "hybrid_fourier_gqa_shortconv_dit"


def rmsnorm(x, g):
    xf = x.astype(jnp.float32)
    y = xf * lax.rsqrt(jnp.mean(xf * xf, axis=-1, keepdims=True) + EPS)
    return (y * g.astype(jnp.float32)).astype(x.dtype)


def rope_axis(x, pos):
    n = x.shape[-1]
    half = n // 2
    freqs = ROPE_THETA ** (-jnp.arange(half, dtype=jnp.float32) / half)
    ang = pos.astype(jnp.float32)[:, None] * freqs[None, :]
    cos = jnp.cos(ang)[None, :, None, :]
    sin = jnp.sin(ang)[None, :, None, :]
    xf = x.astype(jnp.float32)
    x1, x2 = xf[..., :half], xf[..., half:]
    out = jnp.concatenate([x1 * cos - x2 * sin, x1 * sin + x2 * cos], axis=-1)
    return out.astype(x.dtype)


def rope_2d(x, row, col):
    h = HEAD_DIM // 2
    return jnp.concatenate([rope_axis(x[..., :h], row), rope_axis(x[..., h:], col)], axis=-1)


def attend(q, k, v):
    b, sq, _, dh = q.shape
    g = N_HEADS // N_KV_HEADS
    nb = sq // Q_BLOCK
    scale = 1.0 / float(np.sqrt(dh))
    qb = q.reshape(b, nb, Q_BLOCK, N_KV_HEADS, g, dh).transpose(1, 0, 3, 4, 2, 5)
    kt = k.transpose(0, 2, 1, 3)
    vt = v.transpose(0, 2, 1, 3)

    def block(qi):
        s = jnp.einsum('bkgqd,bkld->bkgql', qi, kt).astype(jnp.float32) * scale
        p = jax.nn.softmax(s, axis=-1).astype(vt.dtype)
        return jnp.einsum('bkgql,bkld->bkgqd', p, vt)

    o = lax.map(block, qb)
    return o.transpose(1, 0, 4, 2, 3, 5).reshape(b, sq, N_HEADS * dh)


def fourier_mix(u):
    b, s, _ = u.shape
    ug = u.reshape(b, s, F_GROUPS, F_GROUP_DIM).astype(jnp.float32)
    y = jnp.fft.fft2(ug, axes=(1, 3), norm="ortho").real
    return y.reshape(b, s, F_DIM).astype(u.dtype)


def short_conv(u, w, bias):
    up = jnp.pad(u, ((0, 0), (1, 1), (0, 0)))
    return up[:, :-2] * w[0] + up[:, 1:-1] * w[1] + up[:, 2:] * w[2] + bias


def heads_qk(q, k, qn, kn):
    b, s, _ = q.shape
    q = rmsnorm(q.reshape(b, s, N_HEADS, HEAD_DIM), qn)
    k = rmsnorm(k.reshape(b, s, N_KV_HEADS, HEAD_DIM), kn)
    return q, k


def merge_branches(attn, ag, fx, fg, cx, cb, cc, cg, ml,
                   w_attn_o, w_f_mix, w_f_o, conv_w, conv_b, w_conv_o, w_out):
    silu = jax.nn.silu
    y_a = (attn * silu(ag)) @ w_attn_o
    y_f = ((fourier_mix(fx) @ w_f_mix) * silu(fg)) @ w_f_o
    y_c = (cb * short_conv(cc * cx, conv_w, conv_b) * silu(cg)) @ w_conv_o
    g = jax.nn.sigmoid(ml.astype(jnp.float32)).astype(ml.dtype)
    g = g.reshape(ml.shape[:-1] + (N_BRANCH, D_MODEL))
    m = g[..., 0, :] * y_a + g[..., 1, :] * y_f + g[..., 2, :] * y_c
    return m @ w_out


def setup_inputs(seed: int = 0) -> dict:
    key = jax.random.key(seed)
    ks = jax.random.split(key, 20)
    f32 = jnp.float32
    D = D_MODEL

    def nrm(k, shape, scale):
        return jax.random.normal(k, shape, f32) * scale

    return {
        "x": nrm(ks[0], (BATCH, SEQ, D), 1.0),
        "c": nrm(ks[1], (BATCH, D), 1.0),
        "ctx": nrm(ks[2], (BATCH, CTX_LEN, D), 1.0),
        "c_ctx": nrm(ks[3], (D,), 1.0),
        "w_mod": nrm(ks[4], (DEPTH, D, 3 * D), 0.5 * D ** -0.5),
        "b_mod": nrm(ks[5], (DEPTH, 3 * D), 0.01),
        "g_pre": 1.0 + nrm(ks[6], (DEPTH, D), 0.01),
        "g_post": 1.0 + nrm(ks[7], (DEPTH, D), 0.01),
        "w_in": nrm(ks[8], (DEPTH, D, IN_COLS), D ** -0.5),
        "q_norm": 1.0 + nrm(ks[9], (DEPTH, HEAD_DIM), 0.01),
        "k_norm": 1.0 + nrm(ks[10], (DEPTH, HEAD_DIM), 0.01),
        "w_attn_o": nrm(ks[11], (DEPTH, ATTN_DIM, D), ATTN_DIM ** -0.5),
        "w_f_mix": nrm(ks[12], (DEPTH, F_DIM, F_DIM), F_DIM ** -0.5),
        "w_f_o": nrm(ks[13], (DEPTH, F_DIM, D), F_DIM ** -0.5),
        "conv_w": nrm(ks[14], (DEPTH, CONV_WIDTH, CONV_DIM), CONV_WIDTH ** -0.5),
        "conv_b": nrm(ks[15], (DEPTH, CONV_DIM), 0.01),
        "w_conv_o": nrm(ks[16], (DEPTH, CONV_DIM, D), CONV_DIM ** -0.5),
        "w_out": nrm(ks[17], (DEPTH, D, D), D ** -0.5),
    }


def reference(x, c, ctx, c_ctx, w_mod, b_mod, g_pre, g_post, w_in, q_norm, k_norm,
              w_attn_o, w_f_mix, w_f_o, conv_w, conv_b, w_conv_o, w_out):
    S = x.shape[1]
    ROWS = S // GRID_W
    row = jnp.repeat(jnp.arange(ROWS, dtype=jnp.int32), GRID_W, total_repeat_length=S)
    col = jnp.tile(jnp.arange(GRID_W, dtype=jnp.int32), ROWS)

    sc_lat = jax.nn.silu(c)
    sc_ctx = jax.nn.silu(c_ctx)[None]
    h_ctx_stream = ctx

    for l in range(DEPTH):
        last = l == DEPTH - 1
        mod = sc_lat @ w_mod[l] + b_mod[l]
        shift, scale, gate = jnp.split(mod, 3, axis=-1)
        mod_c = sc_ctx @ w_mod[l] + b_mod[l]
        shift_c, scale_c, gate_c = jnp.split(mod_c, 3, axis=-1)

        h = rmsnorm(x, g_pre[l]) * (1.0 + scale[:, None, :]) + shift[:, None, :]
        hc = rmsnorm(h_ctx_stream, g_pre[l]) * (1.0 + scale_c[:, None, :]) + shift_c[:, None, :]

        q, k, v, ag, fx, fg, cx, cb, cc, cg, ml = jnp.split(h @ w_in[l], IN_OFFSETS, axis=-1)
        q, k = heads_qk(q, k, q_norm[l], k_norm[l])
        q = rope_2d(q, row, col)
        k = rope_2d(k, row, col)
        v = v.reshape(v.shape[0], S, N_KV_HEADS, HEAD_DIM)

        if last:
            kv_c = hc @ w_in[l][:, KV_START:KV_END]
            k_c, v_c = jnp.split(kv_c, 2, axis=-1)
            k_c = rmsnorm(k_c.reshape(k_c.shape[0], CTX_LEN, N_KV_HEADS, HEAD_DIM), k_norm[l])
        else:
            (q_c, k_c, v_c, ag_c, fx_c, fg_c, cx_c, cb_c, cc_c, cg_c, ml_c) = jnp.split(
                hc @ w_in[l], IN_OFFSETS, axis=-1)
            q_c, k_c = heads_qk(q_c, k_c, q_norm[l], k_norm[l])
        v_c = v_c.reshape(v_c.shape[0], CTX_LEN, N_KV_HEADS, HEAD_DIM)

        k_all = jnp.concatenate([jnp.broadcast_to(k_c, (k.shape[0],) + k_c.shape[1:]), k], axis=1)
        v_all = jnp.concatenate([jnp.broadcast_to(v_c, (v.shape[0],) + v_c.shape[1:]), v], axis=1)
        attn = attend(q, k_all, v_all)
        out = merge_branches(attn, ag, fx, fg, cx, cb, cc, cg, ml,
                             w_attn_o[l], w_f_mix[l], w_f_o[l], conv_w[l], conv_b[l],
                             w_conv_o[l], w_out[l])

        if not last:
            attn_c = attend(q_c, k_c, v_c)
            out_c = merge_branches(attn_c, ag_c, fx_c, fg_c, cx_c, cb_c, cc_c, cg_c, ml_c,
                                   w_attn_o[l], w_f_mix[l], w_f_o[l], conv_w[l], conv_b[l],
                                   w_conv_o[l], w_out[l])
            h_ctx_stream = h_ctx_stream + gate_c[:, None, :] * rmsnorm(out_c, g_post[l])

        x = x + gate[:, None, :] * rmsnorm(out, g_post[l])

    return x
```

```python
import functools

import jax
import jax.numpy as jnp
from jax import lax
from jax.experimental import pallas as pl
from jax.experimental.pallas import tpu as pltpu

D = 2048
SEQ = 8192
CTX = 256
S_ALL = SEQ + CTX
DEPTH = 4
GRID_W = 64
HEAD_DIM = 128
N_HEADS = 16
N_KV = 4
Q_PER_KV = N_HEADS // N_KV
KV_DIM = N_KV * HEAD_DIM
F_GROUPS = 4
F_DIM = 1024
F_GROUP_DIM = F_DIM // F_GROUPS
CONV_DIM = 1024
ROPE_THETA = 10000.0
EPS = 1e-6
IN_COLS = 17408
QKV_COLS = D + 2 * KV_DIM
REST_COLS = IN_COLS - QKV_COLS
OFF_AG, OFF_FX, OFF_FG, OFF_CX, OFF_CB, OFF_CC, OFF_CG, OFF_ML = (
    0, 2048, 3072, 4096, 5120, 6144, 7168, 8192)

TM = 1056
TM_S = 528
VMEM_LIMIT_BYTES = 56 * 1024 * 1024

F32 = jnp.float32
BF16 = jnp.bfloat16


def _params(*sem):
    return pltpu.CompilerParams(dimension_semantics=sem, vmem_limit_bytes=VMEM_LIMIT_BYTES)


def _sigmoid(v):
    return 1.0 / (1.0 + jnp.exp(-v))


def _silu(v):
    return v * _sigmoid(v)


def _row_ids(i, tm):
    return i * tm + lax.broadcasted_iota(jnp.int32, (tm, 1), 0)


def _mod_kernel(c_ref, w_ref, b_ref, o_ref):
    sc = _silu(c_ref[...])
    o_ref[0] = jnp.dot(sc.astype(BF16), w_ref[0], preferred_element_type=F32) + b_ref[0]


def _modulation(c8, w_mod, b_mod):
    tn = 1024
    return pl.pallas_call(
        _mod_kernel,
        out_shape=jax.ShapeDtypeStruct((DEPTH, 8, 3 * D), F32),
        grid=(DEPTH, 3 * D // tn),
        in_specs=[
            pl.BlockSpec((8, D), lambda l, j: (0, 0)),
            pl.BlockSpec((1, D, tn), lambda l, j: (l, 0, j)),
            pl.BlockSpec((1, 1, tn), lambda l, j: (l, 0, j)),
        ],
        out_specs=pl.BlockSpec((1, 8, tn), lambda l, j: (l, 0, j)),
        compiler_params=_params("arbitrary", "arbitrary"),
        name="modulation",
    )(c8, w_mod, b_mod)


def _prenorm_kernel(x_ref, mod_ref, g_ref, h_ref, *, tm):
    x = x_ref[...]
    ms = jnp.mean(x * x, axis=-1, keepdims=True)
    y = x * lax.rsqrt(ms + EPS) * g_ref[0]
    m = mod_ref[0]
    is_lat = _row_ids(pl.program_id(0), tm) < SEQ
    shift = jnp.where(is_lat, m[0:1, 0:D], m[1:2, 0:D])
    scale = jnp.where(is_lat, m[0:1, D:2 * D], m[1:2, D:2 * D])
    h_ref[...] = (y * (1.0 + scale) + shift).astype(BF16)


def _prenorm(x_all, mod, g_pre, l):
    tm = TM_S
    return pl.pallas_call(
        functools.partial(_prenorm_kernel, tm=tm),
        out_shape=jax.ShapeDtypeStruct((S_ALL, D), BF16),
        grid=(S_ALL // tm,),
        in_specs=[
            pl.BlockSpec((tm, D), lambda i: (i, 0)),
            pl.BlockSpec((1, 8, 3 * D), lambda i: (l, 0, 0)),
            pl.BlockSpec((1, 1, D), lambda i: (l, 0, 0)),
        ],
        out_specs=pl.BlockSpec((tm, D), lambda i: (i, 0)),
        compiler_params=_params("arbitrary"),
        name="prenorm",
    )(x_all, mod, g_pre)


def _swap_halves(t):
    lane = lax.broadcasted_iota(jnp.int32, t.shape, 1)
    return jnp.where((lane % 64) < 32, pltpu.roll(t, 96, 1), pltpu.roll(t, 32, 1))


def _inproj_qkv_kernel(h_ref, w_ref, nw_ref, cos_ref, sin_ref, o_ref, *, tn):
    j = pl.program_id(1)
    acc = jnp.dot(h_ref[...], w_ref[0], preferred_element_type=F32)

    @pl.when(j * tn < D + KV_DIM)
    def _():
        cos = cos_ref[...]
        sin = sin_ref[...]
        for hh in range(tn // HEAD_DIM):
            sl = slice(hh * HEAD_DIM, (hh + 1) * HEAD_DIM)
            t = acc[:, sl]
            ms = jnp.mean(t * t, axis=-1, keepdims=True)
            y = t * lax.rsqrt(ms + EPS) * nw_ref[:, sl]
            o_ref[:, sl] = (y * cos + _swap_halves(y) * sin).astype(BF16)

    @pl.when(j * tn >= D + KV_DIM)
    def _():
        o_ref[...] = acc.astype(BF16)


def _inproj_qkv(h, w_in, nw, cos_tab, sin_tab, l):
    tn = 512
    return pl.pallas_call(
        functools.partial(_inproj_qkv_kernel, tn=tn),
        out_shape=jax.ShapeDtypeStruct((S_ALL, QKV_COLS), BF16),
        grid=(S_ALL // TM, QKV_COLS // tn),
        in_specs=[
            pl.BlockSpec((TM, D), lambda i, j: (i, 0)),
            pl.BlockSpec((1, D, tn), lambda i, j: (l, 0, j)),
            pl.BlockSpec((1, tn), lambda i, j: (0, j)),
            pl.BlockSpec((TM, HEAD_DIM), lambda i, j: (i, 0)),
            pl.BlockSpec((TM, HEAD_DIM), lambda i, j: (i, 0)),
        ],
        out_specs=pl.BlockSpec((TM, tn), lambda i, j: (i, j)),
        compiler_params=_params("arbitrary", "arbitrary"),
        name="inproj_qkv",
    )(h, w_in, nw, cos_tab, sin_tab)


def _inproj_rest_kernel(h_ref, w_ref, o_ref):
    o_ref[...] = jnp.dot(h_ref[...], w_ref[0], preferred_element_type=F32)


def _inproj_rest(h, w_in, l):
    tn = 1024
    off = QKV_COLS // tn
    return pl.pallas_call(
        _inproj_rest_kernel,
        out_shape=jax.ShapeDtypeStruct((S_ALL, REST_COLS), F32),
        grid=(S_ALL // TM, REST_COLS // tn),
        in_specs=[
            pl.BlockSpec((TM, D), lambda i, j: (i, 0)),
            pl.BlockSpec((1, D, tn), lambda i, j: (l, 0, off + j)),
        ],
        out_specs=pl.BlockSpec((TM, tn), lambda i, j: (i, j)),
        compiler_params=_params("arbitrary", "arbitrary"),
        name="inproj_rest",
    )(h, w_in)


def _attn_kernel(q_ref, k_ref, v_ref, o_ref, q4_ref, m_ref, l_ref, acc_ref, *, tq, tk):
    qi = pl.program_id(1)
    for hh in range(Q_PER_KV):
        q4_ref[hh * tq:(hh + 1) * tq, :] = q_ref[:, hh * HEAD_DIM:(hh + 1) * HEAD_DIM]
    m_ref[...] = jnp.full(m_ref.shape, -jnp.inf, F32)
    l_ref[...] = jnp.zeros(l_ref.shape, F32)
    acc_ref[...] = jnp.zeros(acc_ref.shape, F32)
    sm_scale = 1.0 / float(HEAD_DIM) ** 0.5

    def step(start, size):
        k = k_ref[pl.ds(start, size), :]
        v = v_ref[pl.ds(start, size), :]
        s = lax.dot_general(q4_ref[...], k, (((1,), (1,)), ((), ())),
                            preferred_element_type=F32) * sm_scale
        m_prev = m_ref[...]
        m_new = jnp.maximum(m_prev, jnp.max(s, axis=-1, keepdims=True))
        alpha = jnp.exp(m_prev - m_new)
        p = jnp.exp(s - m_new)
        l_ref[...] = alpha * l_ref[...] + jnp.sum(p, axis=-1, keepdims=True)
        acc_ref[...] = alpha * acc_ref[...] + jnp.dot(p.astype(BF16), v, preferred_element_type=F32)
        m_ref[...] = m_new

    @pl.when(qi < SEQ // tq)
    def _():
        def body(c, carry):
            step(pl.multiple_of(c * tk, tk), tk)
            return carry
        lax.fori_loop(0, SEQ // tk, body, 0)

    step(SEQ, CTX)
    o = acc_ref[...] / l_ref[...]
    for hh in range(Q_PER_KV):
        o_ref[:, hh * HEAD_DIM:(hh + 1) * HEAD_DIM] = o[hh * tq:(hh + 1) * tq, :]


def _attention(qkv):
    tq, tk = 256, 1024
    gw = Q_PER_KV * HEAD_DIM
    k_blk = D // HEAD_DIM
    v_blk = (D + KV_DIM) // HEAD_DIM
    return pl.pallas_call(
        functools.partial(_attn_kernel, tq=tq, tk=tk),
        out_shape=jax.ShapeDtypeStruct((S_ALL, D), F32),
        grid=(N_KV, S_ALL // tq),
        in_specs=[
            pl.BlockSpec((tq, gw), lambda g, qi: (qi, g)),
            pl.BlockSpec((S_ALL, HEAD_DIM), lambda g, qi: (0, k_blk + g)),
            pl.BlockSpec((S_ALL, HEAD_DIM), lambda g, qi: (0, v_blk + g)),
        ],
        out_specs=pl.BlockSpec((tq, gw), lambda g, qi: (qi, g)),
        scratch_shapes=[
            pltpu.VMEM((Q_PER_KV * tq, HEAD_DIM), BF16),
            pltpu.VMEM((Q_PER_KV * tq, 1), F32),
            pltpu.VMEM((Q_PER_KV * tq, 1), F32),
            pltpu.VMEM((Q_PER_KV * tq, HEAD_DIM), F32),
        ],
        compiler_params=_params("arbitrary", "arbitrary"),
        name="attention",
    )(qkv, qkv, qkv)


def _four_a_kernel(u_ref, w_ref, pc_ref, ps_ref):
    r = jnp.dot(u_ref[...].astype(BF16), w_ref[...], preferred_element_type=F32)
    pc_ref[...] = r[:, :F_GROUP_DIM].astype(BF16)
    ps_ref[...] = r[:, F_GROUP_DIM:].astype(BF16)


def _fourier_a(rest, w_c):
    off = OFF_FX // F_GROUP_DIM
    out = jax.ShapeDtypeStruct((S_ALL, F_DIM), BF16)
    return pl.pallas_call(
        _four_a_kernel,
        out_shape=(out, out),
        grid=(S_ALL // TM, F_GROUPS),
        in_specs=[
            pl.BlockSpec((TM, F_GROUP_DIM), lambda i, g: (i, off + g)),
            pl.BlockSpec((F_GROUP_DIM, 2 * F_GROUP_DIM), lambda i, g: (0, 0)),
        ],
        out_specs=(pl.BlockSpec((TM, F_GROUP_DIM), lambda i, g: (i, g)),
                   pl.BlockSpec((TM, F_GROUP_DIM), lambda i, g: (i, g))),
        compiler_params=_params("arbitrary", "arbitrary"),
        name="fourier_a",
    )(rest, w_c)


def _four_b_kernel(d_ref, pc_ref, ps_ref, o_ref, acc_ref, *, nk, tm):
    i = pl.program_id(0)
    k = pl.program_id(1)

    @pl.when(k == 0)
    def _():
        acc_ref[...] = jnp.zeros(acc_ref.shape, F32)

    @pl.when(k < nk)
    def _():
        acc_ref[...] += jnp.dot(d_ref[...], pc_ref[...], preferred_element_type=F32)

    @pl.when(k >= nk)
    def _():
        acc_ref[...] += jnp.dot(d_ref[...], ps_ref[...], preferred_element_type=F32)

    @pl.when(k == 2 * nk - 1)
    def _():
        is_lat = _row_ids(i, tm) < SEQ
        scale = jnp.where(is_lat, 1.0 / float(SEQ * F_GROUP_DIM) ** 0.5,
                          1.0 / float(CTX * F_GROUP_DIM) ** 0.5)
        o_ref[...] = (acc_ref[...] * scale).astype(BF16)


def _fourier_b(dft, pc, ps):
    tk = 1408
    nk = S_ALL // tk
    return pl.pallas_call(
        functools.partial(_four_b_kernel, nk=nk, tm=TM),
        out_shape=jax.ShapeDtypeStruct((S_ALL, F_DIM), BF16),
        grid=(S_ALL // TM, 2 * nk),
        in_specs=[
            pl.BlockSpec((TM, tk), lambda i, k: (i, k)),
            pl.BlockSpec((tk, F_DIM), lambda i, k: (jnp.minimum(k, nk - 1), 0)),
            pl.BlockSpec((tk, F_DIM), lambda i, k: (jnp.maximum(k - nk, 0), 0)),
        ],
        out_specs=pl.BlockSpec((TM, F_DIM), lambda i, k: (i, 0)),
        scratch_shapes=[pltpu.VMEM((TM, F_DIM), F32)],
        compiler_params=_params("arbitrary", "arbitrary"),
        name="fourier_b",
    )(dft, pc, ps)


def _proj_a_kernel(attn_ref, ag_ref, w_ref, o_ref, a_ref):
    @pl.when(pl.program_id(1) == 0)
    def _():
        a_ref[...] = (attn_ref[...] * _silu(ag_ref[...])).astype(BF16)

    o_ref[...] = jnp.dot(a_ref[...], w_ref[0], preferred_element_type=F32)


def _proj_a(attn, rest, w_attn_o, l):
    tm, tn = TM_S, 1024
    return pl.pallas_call(
        _proj_a_kernel,
        out_shape=jax.ShapeDtypeStruct((S_ALL, D), F32),
        grid=(S_ALL // tm, D // tn),
        in_specs=[
            pl.BlockSpec((tm, D), lambda i, j: (i, 0)),
            pl.BlockSpec((tm, D), lambda i, j: (i, OFF_AG // D)),
            pl.BlockSpec((1, D, tn), lambda i, j: (l, 0, j)),
        ],
        out_specs=pl.BlockSpec((tm, tn), lambda i, j: (i, j)),
        scratch_shapes=[pltpu.VMEM((tm, D), BF16)],
        compiler_params=_params("arbitrary", "arbitrary"),
        name="proj_attn",
    )(attn, rest, w_attn_o)


def _proj_f_kernel(y_ref, fg_ref, wmix_ref, wo_ref, o_ref):
    t = jnp.dot(y_ref[...], wmix_ref[0], preferred_element_type=F32) * _silu(fg_ref[...])
    o_ref[...] = jnp.dot(t.astype(BF16), wo_ref[0], preferred_element_type=F32)


def _proj_f(fmix, rest, w_f_mix, w_f_o, l):
    tm = TM_S
    return pl.pallas_call(
        _proj_f_kernel,
        out_shape=jax.ShapeDtypeStruct((S_ALL, D), F32),
        grid=(S_ALL // tm,),
        in_specs=[
            pl.BlockSpec((tm, F_DIM), lambda i: (i, 0)),
            pl.BlockSpec((tm, F_DIM), lambda i: (i, OFF_FG // F_DIM)),
            pl.BlockSpec((1, F_DIM, F_DIM), lambda i: (l, 0, 0)),
            pl.BlockSpec((1, F_DIM, D), lambda i: (l, 0, 0)),
        ],
        out_specs=pl.BlockSpec((tm, D), lambda i: (i, 0)),
        compiler_params=_params("arbitrary"),
        name="proj_fourier",
    )(fmix, rest, w_f_mix, w_f_o)


def _proj_c_kernel(cx_ref, cb_ref, cc_ref, cg_ref, cxp_ref, ccp_ref, cxn_ref, ccn_ref,
                   cw_ref, cbias_ref, w_ref, o_ref, *, tm):
    u = cc_ref[...] * cx_ref[...]
    u_before = ccp_ref[7:8, :] * cxp_ref[7:8, :]
    u_after = ccn_ref[0:1, :] * cxn_ref[0:1, :]
    rows = _row_ids(pl.program_id(0), tm)
    loc = lax.broadcasted_iota(jnp.int32, (tm, 1), 0)
    u_prev = jnp.where(loc == 0, u_before, pltpu.roll(u, 1, 0))
    u_prev = jnp.where((rows == 0) | (rows == SEQ), 0.0, u_prev)
    u_next = jnp.where(loc == tm - 1, u_after, pltpu.roll(u, tm - 1, 0))
    u_next = jnp.where((rows == SEQ - 1) | (rows == S_ALL - 1), 0.0, u_next)
    cw = cw_ref[0]
    conv = u_prev * cw[0:1, :] + u * cw[1:2, :] + u_next * cw[2:3, :] + cbias_ref[0]
    t = cb_ref[...] * conv * _silu(cg_ref[...])
    o_ref[...] = jnp.dot(t.astype(BF16), w_ref[0], preferred_element_type=F32)


def _proj_c(rest, conv_w, conv_b, w_conv_o, l):
    tm = TM_S
    rb = tm // 8
    last = S_ALL // 8 - 1

    def col(off):
        return pl.BlockSpec((tm, CONV_DIM), lambda i: (i, off // CONV_DIM))

    def above(off):
        return pl.BlockSpec((8, CONV_DIM), lambda i: (jnp.maximum(i * rb - 1, 0), off // CONV_DIM))

    def below(off):
        return pl.BlockSpec((8, CONV_DIM), lambda i: (jnp.minimum((i + 1) * rb, last), off // CONV_DIM))

    return pl.pallas_call(
        functools.partial(_proj_c_kernel, tm=tm),
        out_shape=jax.ShapeDtypeStruct((S_ALL, D), F32),
        grid=(S_ALL // tm,),
        in_specs=[
            col(OFF_CX), col(OFF_CB), col(OFF_CC), col(OFF_CG),
            above(OFF_CX), above(OFF_CC), below(OFF_CX), below(OFF_CC),
            pl.BlockSpec((1, 3, CONV_DIM), lambda i: (l, 0, 0)),
            pl.BlockSpec((1, 1, CONV_DIM), lambda i: (l, 0, 0)),
            pl.BlockSpec((1, CONV_DIM, D), lambda i: (l, 0, 0)),
        ],
        out_specs=pl.BlockSpec((tm, D), lambda i: (i, 0)),
        compiler_params=_params("arbitrary"),
        name="proj_conv",
    )(rest, rest, rest, rest, rest, rest, rest, rest, conv_w, conv_b, w_conv_o)


def _final_kernel(ml0_ref, ml1_ref, ml2_ref, ya_ref, yf_ref, yc_ref, w_ref, x_ref,
                  mod_ref, gp_ref, o_ref, acc_ref, *, tm, nj):
    j = pl.program_id(1)
    m = (_sigmoid(ml0_ref[...]) * ya_ref[...] + _sigmoid(ml1_ref[...]) * yf_ref[...]
         + _sigmoid(ml2_ref[...]) * yc_ref[...])
    part = jnp.dot(m.astype(BF16), w_ref[0], preferred_element_type=F32)

    @pl.when(j == 0)
    def _():
        acc_ref[...] = part

    @pl.when(j > 0)
    def _():
        acc_ref[...] += part

    @pl.when(j == nj - 1)
    def _():
        out = acc_ref[...]
        ms = jnp.mean(out * out, axis=-1, keepdims=True)
        r = out * lax.rsqrt(ms + EPS) * gp_ref[0]
        md = mod_ref[0]
        is_lat = _row_ids(pl.program_id(0), tm) < SEQ
        gate = jnp.where(is_lat, md[0:1, 2 * D:3 * D], md[1:2, 2 * D:3 * D])
        o_ref[...] = x_ref[...] + gate * r


def _final(rest, ya, yf, yc, w_out, x_all, mod, g_post, l):
    tm, tn = TM_S, 512
    nj = D // tn

    def ml(b):
        base = (OFF_ML + b * D) // tn
        return pl.BlockSpec((tm, tn), lambda i, j: (i, base + j))

    ysp = pl.BlockSpec((tm, tn), lambda i, j: (i, j))
    return pl.pallas_call(
        functools.partial(_final_kernel, tm=tm, nj=nj),
        out_shape=jax.ShapeDtypeStruct((S_ALL, D), F32),
        grid=(S_ALL // tm, nj),
        in_specs=[
            ml(0), ml(1), ml(2), ysp, ysp, ysp,
            pl.BlockSpec((1, tn, D), lambda i, j: (l, j, 0)),
            pl.BlockSpec((tm, D), lambda i, j: (i, 0)),
            pl.BlockSpec((1, 8, 3 * D), lambda i, j: (l, 0, 0)),
            pl.BlockSpec((1, 1, D), lambda i, j: (l, 0, 0)),
        ],
        out_specs=pl.BlockSpec((tm, D), lambda i, j: (i, 0)),
        scratch_shapes=[pltpu.VMEM((tm, D), F32)],
        compiler_params=_params("arbitrary", "arbitrary"),
        name="merge_out",
    )(rest, rest, rest, ya, yf, yc, w_out, x_all, mod, g_post)


def _rope_tables():
    half = HEAD_DIM // 4
    pos = jnp.arange(SEQ, dtype=jnp.int32)
    freqs = ROPE_THETA ** (-jnp.arange(half, dtype=F32) / half)
    ang_r = (pos // GRID_W).astype(F32)[:, None] * freqs[None, :]
    ang_c = (pos % GRID_W).astype(F32)[:, None] * freqs[None, :]
    cos = jnp.concatenate([jnp.cos(ang_r), jnp.cos(ang_r), jnp.cos(ang_c), jnp.cos(ang_c)], axis=-1)
    sin = jnp.concatenate([-jnp.sin(ang_r), jnp.sin(ang_r), -jnp.sin(ang_c), jnp.sin(ang_c)], axis=-1)
    cos = jnp.concatenate([cos, jnp.ones((CTX, HEAD_DIM), F32)], axis=0)
    sin = jnp.concatenate([sin, jnp.zeros((CTX, HEAD_DIM), F32)], axis=0)
    return cos, sin


def _dft_cos_sin(n):
    idx = jnp.arange(n, dtype=jnp.int32)
    ang = ((idx[:, None] * idx[None, :]) % n).astype(F32) * (2.0 * jnp.pi / n)
    return jnp.cos(ang), jnp.sin(ang)


def _dft_tables():
    c_s, s_s = _dft_cos_sin(SEQ)
    c_x, s_x = _dft_cos_sin(CTX)
    z_lx = jnp.zeros((SEQ, CTX), BF16)
    z_xl = jnp.zeros((CTX, SEQ), BF16)
    top = jnp.concatenate([c_s.astype(BF16), z_lx, (-s_s).astype(BF16), z_lx], axis=1)
    bot = jnp.concatenate([z_xl, c_x.astype(BF16), z_xl, (-s_x).astype(BF16)], axis=1)
    dft = jnp.concatenate([top, bot], axis=0)
    c_c, s_c = _dft_cos_sin(F_GROUP_DIM)
    w_c = jnp.concatenate([c_c, s_c], axis=1).astype(BF16)
    return dft, w_c


def kernel(x, c, ctx, c_ctx, w_mod, b_mod, g_pre, g_post, w_in, q_norm, k_norm,
           w_attn_o, w_f_mix, w_f_o, conv_w, conv_b, w_conv_o, w_out):
    assert x.shape == (1, SEQ, D) and ctx.shape == (1, CTX, D) and w_in.shape == (DEPTH, D, IN_COLS)
    x_all = jnp.concatenate([x[0], ctx[0]], axis=0)
    c8 = jnp.concatenate([c, c_ctx[None, :], jnp.zeros((6, D), F32)], axis=0)
    cos_tab, sin_tab = _rope_tables()
    dft, w_c = _dft_tables()

    w_in_b = w_in.astype(BF16)
    w_attn_o_b = w_attn_o.astype(BF16)
    w_f_mix_b = w_f_mix.astype(BF16)
    w_f_o_b = w_f_o.astype(BF16)
    w_conv_o_b = w_conv_o.astype(BF16)
    w_out_b = w_out.astype(BF16)

    mod = _modulation(c8, w_mod.astype(BF16), b_mod.reshape(DEPTH, 1, 3 * D))
    g_pre3 = g_pre.reshape(DEPTH, 1, D)
    g_post3 = g_post.reshape(DEPTH, 1, D)
    conv_b3 = conv_b.reshape(DEPTH, 1, CONV_DIM)

    for l in range(DEPTH):
        nw = jnp.concatenate([jnp.tile(q_norm[l], N_HEADS), jnp.tile(k_norm[l], N_KV),
                              jnp.ones((KV_DIM,), F32)])[None, :]
        h = _prenorm(x_all, mod, g_pre3, l)
        qkv = _inproj_qkv(h, w_in_b, nw, cos_tab, sin_tab, l)
        rest = _inproj_rest(h, w_in_b, l)
        attn = _attention(qkv)
        pc, ps = _fourier_a(rest, w_c)
        fmix = _fourier_b(dft, pc, ps)
        ya = _proj_a(attn, rest, w_attn_o_b, l)
        yf = _proj_f(fmix, rest, w_f_mix_b, w_f_o_b, l)
        yc = _proj_c(rest, conv_w, conv_b3, w_conv_o_b, l)
        x_all = _final(rest, ya, yf, yc, w_out_b, x_all, mod, g_post3, l)

    return x_all[:SEQ][None]
```

```python
import functools

import jax
import jax.numpy as jnp
from jax import lax
from jax.experimental import pallas as pl
from jax.experimental.pallas import tpu as pltpu

D = 2048
SEQ = 8192
CTX = 256
S_ALL = SEQ + CTX
DEPTH = 4
GRID_W = 64
HEAD_DIM = 128
N_HEADS = 16
N_KV = 4
Q_PER_KV = N_HEADS // N_KV
KV_DIM = N_KV * HEAD_DIM
F_GROUPS = 4
F_DIM = 1024
F_GROUP_DIM = F_DIM // F_GROUPS
CONV_DIM = 1024
ROPE_THETA = 10000.0
EPS = 1e-6
QK_PRESCALE = 1.4426950408889634 / float(HEAD_DIM) ** 0.5
IN_COLS = 17408
QKV_COLS = D + 2 * KV_DIM
REST_COLS = IN_COLS - QKV_COLS
OFF_AG, OFF_FX, OFF_FG, OFF_CX, OFF_CB, OFF_CC, OFF_CG, OFF_ML = (
    0, 2048, 3072, 4096, 5120, 6144, 7168, 8192)

TM = 1056
TM_S = 528
VMEM_LIMIT_BYTES = 56 * 1024 * 1024

F32 = jnp.float32
BF16 = jnp.bfloat16


def _params(*sem):
    return pltpu.CompilerParams(dimension_semantics=sem, vmem_limit_bytes=VMEM_LIMIT_BYTES)


def _sigmoid(v):
    return 1.0 / (1.0 + jnp.exp(-v))


def _silu(v):
    return v * _sigmoid(v)


def _row_ids(i, tm):
    return i * tm + lax.broadcasted_iota(jnp.int32, (tm, 1), 0)


def _mod_kernel(c_ref, w_ref, b_ref, o_ref):
    sc = _silu(c_ref[...])
    o_ref[0] = jnp.dot(sc.astype(BF16), w_ref[0], preferred_element_type=F32) + b_ref[0]


def _modulation(c8, w_mod, b_mod):
    tn = 1024
    return pl.pallas_call(
        _mod_kernel,
        out_shape=jax.ShapeDtypeStruct((DEPTH, 8, 3 * D), F32),
        grid=(DEPTH, 3 * D // tn),
        in_specs=[
            pl.BlockSpec((8, D), lambda l, j: (0, 0)),
            pl.BlockSpec((1, D, tn), lambda l, j: (l, 0, j)),
            pl.BlockSpec((1, 1, tn), lambda l, j: (l, 0, j)),
        ],
        out_specs=pl.BlockSpec((1, 8, tn), lambda l, j: (l, 0, j)),
        compiler_params=_params("arbitrary", "arbitrary"),
        name="modulation",
    )(c8, w_mod, b_mod)


def _prenorm_kernel(x_ref, mod_ref, g_ref, h_ref, *, tm):
    x = x_ref[...]
    ms = jnp.mean(x * x, axis=-1, keepdims=True)
    y = x * lax.rsqrt(ms + EPS) * g_ref[0]
    m = mod_ref[0]
    is_lat = _row_ids(pl.program_id(0), tm) < SEQ
    shift = jnp.where(is_lat, m[0:1, 0:D], m[1:2, 0:D])
    scale = jnp.where(is_lat, m[0:1, D:2 * D], m[1:2, D:2 * D])
    h_ref[...] = (y * (1.0 + scale) + shift).astype(BF16)


def _prenorm(x_all, mod, g_pre, l):
    tm = TM_S
    return pl.pallas_call(
        functools.partial(_prenorm_kernel, tm=tm),
        out_shape=jax.ShapeDtypeStruct((S_ALL, D), BF16),
        grid=(S_ALL // tm,),
        in_specs=[
            pl.BlockSpec((tm, D), lambda i: (i, 0)),
            pl.BlockSpec((1, 8, 3 * D), lambda i: (l, 0, 0)),
            pl.BlockSpec((1, 1, D), lambda i: (l, 0, 0)),
        ],
        out_specs=pl.BlockSpec((tm, D), lambda i: (i, 0)),
        compiler_params=_params("arbitrary"),
        name="prenorm",
    )(x_all, mod, g_pre)


def _swap_halves(t):
    lane = lax.broadcasted_iota(jnp.int32, t.shape, 1)
    return jnp.where((lane % 64) < 32, pltpu.roll(t, 96, 1), pltpu.roll(t, 32, 1))


def _inproj_qkv_kernel(h_ref, w_ref, nw_ref, cos_ref, sin_ref, o_ref, *, tn):
    j = pl.program_id(1)
    acc = jnp.dot(h_ref[...], w_ref[0], preferred_element_type=F32)

    @pl.when(j * tn < D + KV_DIM)
    def _():
        cos = cos_ref[...]
        sin = sin_ref[...]
        for hh in range(tn // HEAD_DIM):
            sl = slice(hh * HEAD_DIM, (hh + 1) * HEAD_DIM)
            t = acc[:, sl]
            ms = jnp.mean(t * t, axis=-1, keepdims=True)
            y = t * lax.rsqrt(ms + EPS) * nw_ref[:, sl]
            o_ref[:, sl] = (y * cos + _swap_halves(y) * sin).astype(BF16)

    @pl.when(j * tn >= D + KV_DIM)
    def _():
        o_ref[...] = acc.astype(BF16)


def _inproj_qkv(h, w_in, nw, cos_tab, sin_tab, l):
    tn = 512
    return pl.pallas_call(
        functools.partial(_inproj_qkv_kernel, tn=tn),
        out_shape=jax.ShapeDtypeStruct((S_ALL, QKV_COLS), BF16),
        grid=(S_ALL // TM, QKV_COLS // tn),
        in_specs=[
            pl.BlockSpec((TM, D), lambda i, j: (i, 0)),
            pl.BlockSpec((1, D, tn), lambda i, j: (l, 0, j)),
            pl.BlockSpec((1, tn), lambda i, j: (0, j)),
            pl.BlockSpec((TM, HEAD_DIM), lambda i, j: (i, 0)),
            pl.BlockSpec((TM, HEAD_DIM), lambda i, j: (i, 0)),
        ],
        out_specs=pl.BlockSpec((TM, tn), lambda i, j: (i, j)),
        compiler_params=_params("arbitrary", "arbitrary"),
        name="inproj_qkv",
    )(h, w_in, nw, cos_tab, sin_tab)


def _inproj_rest_kernel(h_ref, w_ref, o_ref):
    o_ref[...] = jnp.dot(h_ref[...], w_ref[0], preferred_element_type=F32)


def _inproj_rest(h, w_in, l):
    tn = 1024
    off = QKV_COLS // tn
    return pl.pallas_call(
        _inproj_rest_kernel,
        out_shape=jax.ShapeDtypeStruct((S_ALL, REST_COLS), F32),
        grid=(S_ALL // TM, REST_COLS // tn),
        in_specs=[
            pl.BlockSpec((TM, D), lambda i, j: (i, 0)),
            pl.BlockSpec((1, D, tn), lambda i, j: (l, 0, off + j)),
        ],
        out_specs=pl.BlockSpec((TM, tn), lambda i, j: (i, j)),
        compiler_params=_params("arbitrary", "arbitrary"),
        name="inproj_rest",
    )(h, w_in)


def _attn_kernel(bound_ref, q_ref, k_ref, v_ref, o_ref, q4_ref, m_ref, l_ref, acc_ref,
                 *, tq, tk, groups, bounded):
    qi = pl.program_id(1)
    for hh in range(Q_PER_KV):
        q4_ref[hh * tq:(hh + 1) * tq, :] = q_ref[:, hh * HEAD_DIM:(hh + 1) * HEAD_DIM]
    if not bounded:
        m_ref[...] = jnp.full(m_ref.shape, -jnp.inf, F32)
    l_ref[...] = jnp.zeros(l_ref.shape, F32)
    acc_ref[...] = jnp.zeros(acc_ref.shape, F32)
    rows = Q_PER_KV * tq // groups

    def step(start, size):
        k = k_ref[pl.ds(start, size), :]
        v = v_ref[pl.ds(start, size), :]
        rep = size // HEAD_DIM
        scores = [lax.dot_general(q4_ref[r * rows:(r + 1) * rows, :], k, (((1,), (1,)), ((), ())),
                                  preferred_element_type=F32) for r in range(groups)]
        for r in range(groups):
            rs = slice(r * rows, (r + 1) * rows)
            s = scores[r]
            if bounded:
                p = jnp.exp2(s - bound_ref[0])
                lp = p[:, 0:HEAD_DIM]
                for t in range(1, rep):
                    lp = lp + p[:, t * HEAD_DIM:(t + 1) * HEAD_DIM]
                l_ref[rs, :] += lp
                acc_ref[rs, :] += jnp.dot(p.astype(BF16), v, preferred_element_type=F32)
            else:
                m_prev = m_ref[rs, :]
                m_new = jnp.maximum(m_prev, jnp.max(s, axis=-1, keepdims=True))
                alpha = jnp.exp2(m_prev - m_new)
                p = jnp.exp2(s - pltpu.repeat(m_new, rep, 1))
                l_ref[rs, :] = alpha * l_ref[rs, :] + jnp.sum(p, axis=-1, keepdims=True)
                acc_ref[rs, :] = alpha * acc_ref[rs, :] + jnp.dot(p.astype(BF16), v,
                                                                  preferred_element_type=F32)
                m_ref[rs, :] = m_new

    @pl.when(qi < SEQ // tq)
    def _():
        def body(c, carry):
            step(pl.multiple_of(c * tk, tk), tk)
            return carry
        lax.fori_loop(0, SEQ // tk, body, 0)

    step(SEQ, CTX)
    l = jnp.sum(l_ref[...], axis=-1, keepdims=True) if bounded else l_ref[...]
    o = acc_ref[...] / l
    for hh in range(Q_PER_KV):
        o_ref[:, hh * HEAD_DIM:(hh + 1) * HEAD_DIM] = o[hh * tq:(hh + 1) * tq, :]


def _attention(qkv, bound, bounded):
    tq, tk, groups = 256, 1024, 4
    gw = Q_PER_KV * HEAD_DIM
    k_blk = D // HEAD_DIM
    v_blk = (D + KV_DIM) // HEAD_DIM
    return pl.pallas_call(
        functools.partial(_attn_kernel, tq=tq, tk=tk, groups=groups, bounded=bounded),
        out_shape=jax.ShapeDtypeStruct((S_ALL, D), F32),
        grid=(N_KV, S_ALL // tq),
        in_specs=[
            pl.BlockSpec(memory_space=pltpu.SMEM),
            pl.BlockSpec((tq, gw), lambda g, qi: (qi, g)),
            pl.BlockSpec((S_ALL, HEAD_DIM), lambda g, qi: (0, k_blk + g)),
            pl.BlockSpec((S_ALL, HEAD_DIM), lambda g, qi: (0, v_blk + g)),
        ],
        out_specs=pl.BlockSpec((tq, gw), lambda g, qi: (qi, g)),
        scratch_shapes=[
            pltpu.VMEM((Q_PER_KV * tq, HEAD_DIM), BF16),
            pltpu.VMEM((Q_PER_KV * tq, HEAD_DIM), F32),
            pltpu.VMEM((Q_PER_KV * tq, HEAD_DIM), F32),
            pltpu.VMEM((Q_PER_KV * tq, HEAD_DIM), F32),
        ],
        compiler_params=_params("arbitrary", "arbitrary"),
        name="attention_bounded" if bounded else "attention_online",
    )(bound, qkv, qkv, qkv)


MAX_SCORE_BOUND = 60.0


def _attention_dispatch(qkv, qn, kn):
    bound = (1.02 * HEAD_DIM * QK_PRESCALE) * jnp.max(jnp.abs(qn)) * jnp.max(jnp.abs(kn))
    b1 = bound.reshape(1).astype(F32)
    return lax.cond(bound <= MAX_SCORE_BOUND,
                    lambda: _attention(qkv, b1, True),
                    lambda: _attention(qkv, b1, False))


FFT_R = 8
FFT_M = SEQ // FFT_R


def _four_lat_a_kernel(u_ref, w_ref, zr_ref, zi_ref, r_ref, *, tm):
    r = jnp.dot(u_ref[...].astype(BF16), w_ref[...], preferred_element_type=F32)
    nt = F_GROUP_DIM // 128
    for t in range(2 * nt):
        r_ref[t] = r[:, t * 128:(t + 1) * 128]
    for n1 in range(FFT_R):
        for t in range(2 * nt):
            blk = r_ref[t, pl.ds(n1, tm // FFT_R, stride=FFT_R), :]
            dst = zr_ref if t < nt else zi_ref
            tt = t % nt
            dst[n1, :, tt * 128:(tt + 1) * 128] = blk.astype(BF16)


def _fourier_lat_a(rest, w_c):
    tm = 1024
    off = OFF_FX // F_GROUP_DIM
    out = jax.ShapeDtypeStruct((FFT_R, FFT_M, F_DIM), BF16)
    zspec = pl.BlockSpec((FFT_R, tm // FFT_R, F_GROUP_DIM), lambda i, g: (0, i, g))
    return pl.pallas_call(
        functools.partial(_four_lat_a_kernel, tm=tm),
        out_shape=(out, out),
        grid=(SEQ // tm, F_GROUPS),
        in_specs=[
            pl.BlockSpec((tm, F_GROUP_DIM), lambda i, g: (i, off + g)),
            pl.BlockSpec((F_GROUP_DIM, 2 * F_GROUP_DIM), lambda i, g: (0, 0)),
        ],
        out_specs=(zspec, zspec),
        scratch_shapes=[pltpu.VMEM((2 * F_GROUP_DIM // 128, tm, 128), F32)],
        compiler_params=_params("arbitrary", "arbitrary"),
        name="fourier_lat_a",
    )(rest, w_c)


def _four_lat_b_kernel(w8_ref, c_ref, s_ref, zr_ref, zi_ref, twc_ref, tws_ref, o_ref, acc_ref, *, tc):
    n1 = pl.program_id(1)

    @pl.when(n1 == 0)
    def _():
        acc_ref[...] = jnp.zeros(acc_ref.shape, F32)

    zr = zr_ref[0]
    zi = zi_ref[0]
    cm = c_ref[...]
    sm = s_ref[...]
    vr = (jnp.dot(cm, zr, preferred_element_type=F32) + jnp.dot(sm, zi, preferred_element_type=F32))
    vi = (jnp.dot(cm, zi, preferred_element_type=F32) - jnp.dot(sm, zr, preferred_element_type=F32))
    twc = pltpu.repeat(twc_ref[0], tc // 128, 1)
    tws = pltpu.repeat(tws_ref[0], tc // 128, 1)
    pr = vr * twc + vi * tws
    pi = vi * twc - vr * tws
    for k1 in range(FFT_R):
        acc_ref[k1] += w8_ref[0, n1, k1] * pr + w8_ref[1, n1, k1] * pi

    @pl.when(n1 == FFT_R - 1)
    def _():
        scale = 1.0 / float(SEQ * F_GROUP_DIM) ** 0.5
        for k1 in range(FFT_R):
            o_ref[k1 * FFT_M:(k1 + 1) * FFT_M, :] = (acc_ref[k1] * scale).astype(BF16)


def _fourier_lat_b(w8, c_m, s_m, zr, zi, twc, tws):
    tc = 256
    zspec = pl.BlockSpec((1, FFT_M, tc), lambda j, n1: (n1, 0, j))
    twspec = pl.BlockSpec((1, FFT_M, 128), lambda j, n1: (n1, 0, 0))
    mspec = pl.BlockSpec((FFT_M, FFT_M), lambda j, n1: (0, 0))
    return pl.pallas_call(
        functools.partial(_four_lat_b_kernel, tc=tc),
        out_shape=jax.ShapeDtypeStruct((SEQ, F_DIM), BF16),
        grid=(F_DIM // tc, FFT_R),
        in_specs=[pl.BlockSpec(memory_space=pltpu.SMEM), mspec, mspec, zspec, zspec, twspec, twspec],
        out_specs=pl.BlockSpec((SEQ, tc), lambda j, n1: (0, j)),
        scratch_shapes=[pltpu.VMEM((FFT_R, FFT_M, tc), F32)],
        compiler_params=_params("arbitrary", "arbitrary"),
        name="fourier_lat_b",
    )(w8, c_m, s_m, zr, zi, twc, tws)


def _four_ctx_kernel(u_ref, w_ref, c_ref, s_ref, o_ref):
    z = jnp.dot(u_ref[...].astype(BF16), w_ref[...], preferred_element_type=F32)
    zr = z[:, :F_GROUP_DIM].astype(BF16)
    zi = z[:, F_GROUP_DIM:].astype(BF16)
    y = (jnp.dot(c_ref[...], zr, preferred_element_type=F32)
         + jnp.dot(s_ref[...], zi, preferred_element_type=F32))
    o_ref[...] = (y * (1.0 / float(CTX * F_GROUP_DIM) ** 0.5)).astype(BF16)


def _fourier_ctx(rest, w_c, c_x, s_x):
    off = OFF_FX // F_GROUP_DIM
    mspec = pl.BlockSpec((CTX, CTX), lambda g: (0, 0))
    return pl.pallas_call(
        _four_ctx_kernel,
        out_shape=jax.ShapeDtypeStruct((CTX, F_DIM), BF16),
        grid=(F_GROUPS,),
        in_specs=[
            pl.BlockSpec((CTX, F_GROUP_DIM), lambda g: (SEQ // CTX, off + g)),
            pl.BlockSpec((F_GROUP_DIM, 2 * F_GROUP_DIM), lambda g: (0, 0)),
            mspec, mspec,
        ],
        out_specs=pl.BlockSpec((CTX, F_GROUP_DIM), lambda g: (0, g)),
        compiler_params=_params("arbitrary"),
        name="fourier_ctx",
    )(rest, w_c, c_x, s_x)


def _proj_a_kernel(attn_ref, ag_ref, w_ref, o_ref, a_ref):
    @pl.when(pl.program_id(1) == 0)
    def _():
        a_ref[...] = (attn_ref[...] * _silu(ag_ref[...])).astype(BF16)

    o_ref[...] = jnp.dot(a_ref[...], w_ref[0], preferred_element_type=F32)


def _proj_a(attn, rest, w_attn_o, l):
    tm, tn = TM_S, 1024
    return pl.pallas_call(
        _proj_a_kernel,
        out_shape=jax.ShapeDtypeStruct((S_ALL, D), F32),
        grid=(S_ALL // tm, D // tn),
        in_specs=[
            pl.BlockSpec((tm, D), lambda i, j: (i, 0)),
            pl.BlockSpec((tm, D), lambda i, j: (i, OFF_AG // D)),
            pl.BlockSpec((1, D, tn), lambda i, j: (l, 0, j)),
        ],
        out_specs=pl.BlockSpec((tm, tn), lambda i, j: (i, j)),
        scratch_shapes=[pltpu.VMEM((tm, D), BF16)],
        compiler_params=_params("arbitrary", "arbitrary"),
        name="proj_attn",
    )(attn, rest, w_attn_o)


def _proj_f_kernel(y_ref, fg_ref, wmix_ref, wo_ref, o_ref):
    t = jnp.dot(y_ref[...], wmix_ref[0], preferred_element_type=F32) * _silu(fg_ref[...])
    o_ref[...] = jnp.dot(t.astype(BF16), wo_ref[0], preferred_element_type=F32)


def _proj_f(fmix, rest, w_f_mix, w_f_o, l):
    tm = TM_S
    return pl.pallas_call(
        _proj_f_kernel,
        out_shape=jax.ShapeDtypeStruct((S_ALL, D), F32),
        grid=(S_ALL // tm,),
        in_specs=[
            pl.BlockSpec((tm, F_DIM), lambda i: (i, 0)),
            pl.BlockSpec((tm, F_DIM), lambda i: (i, OFF_FG // F_DIM)),
            pl.BlockSpec((1, F_DIM, F_DIM), lambda i: (l, 0, 0)),
            pl.BlockSpec((1, F_DIM, D), lambda i: (l, 0, 0)),
        ],
        out_specs=pl.BlockSpec((tm, D), lambda i: (i, 0)),
        compiler_params=_params("arbitrary"),
        name="proj_fourier",
    )(fmix, rest, w_f_mix, w_f_o)


def _proj_c_kernel(cx_ref, cb_ref, cc_ref, cg_ref, cxp_ref, ccp_ref, cxn_ref, ccn_ref,
                   cw_ref, cbias_ref, w_ref, o_ref, *, tm):
    u = cc_ref[...] * cx_ref[...]
    u_before = ccp_ref[7:8, :] * cxp_ref[7:8, :]
    u_after = ccn_ref[0:1, :] * cxn_ref[0:1, :]
    rows = _row_ids(pl.program_id(0), tm)
    loc = lax.broadcasted_iota(jnp.int32, (tm, 1), 0)
    u_prev = jnp.where(loc == 0, u_before, pltpu.roll(u, 1, 0))
    u_prev = jnp.where((rows == 0) | (rows == SEQ), 0.0, u_prev)
    u_next = jnp.where(loc == tm - 1, u_after, pltpu.roll(u, tm - 1, 0))
    u_next = jnp.where((rows == SEQ - 1) | (rows == S_ALL - 1), 0.0, u_next)
    cw = cw_ref[0]
    conv = u_prev * cw[0:1, :] + u * cw[1:2, :] + u_next * cw[2:3, :] + cbias_ref[0]
    t = cb_ref[...] * conv * _silu(cg_ref[...])
    o_ref[...] = jnp.dot(t.astype(BF16), w_ref[0], preferred_element_type=F32)


def _proj_c(rest, conv_w, conv_b, w_conv_o, l):
    tm = TM_S
    rb = tm // 8
    last = S_ALL // 8 - 1

    def col(off):
        return pl.BlockSpec((tm, CONV_DIM), lambda i: (i, off // CONV_DIM))

    def above(off):
        return pl.BlockSpec((8, CONV_DIM), lambda i: (jnp.maximum(i * rb - 1, 0), off // CONV_DIM))

    def below(off):
        return pl.BlockSpec((8, CONV_DIM), lambda i: (jnp.minimum((i + 1) * rb, last), off // CONV_DIM))

    return pl.pallas_call(
        functools.partial(_proj_c_kernel, tm=tm),
        out_shape=jax.ShapeDtypeStruct((S_ALL, D), F32),
        grid=(S_ALL // tm,),
        in_specs=[
            col(OFF_CX), col(OFF_CB), col(OFF_CC), col(OFF_CG),
            above(OFF_CX), above(OFF_CC), below(OFF_CX), below(OFF_CC),
            pl.BlockSpec((1, 3, CONV_DIM), lambda i: (l, 0, 0)),
            pl.BlockSpec((1, 1, CONV_DIM), lambda i: (l, 0, 0)),
            pl.BlockSpec((1, CONV_DIM, D), lambda i: (l, 0, 0)),
        ],
        out_specs=pl.BlockSpec((tm, D), lambda i: (i, 0)),
        compiler_params=_params("arbitrary"),
        name="proj_conv",
    )(rest, rest, rest, rest, rest, rest, rest, rest, conv_w, conv_b, w_conv_o)


def _final_kernel(ml0_ref, ml1_ref, ml2_ref, ya_ref, yf_ref, yc_ref, w_ref, x_ref,
                  mod_ref, gp_ref, o_ref, acc_ref, *, tm, nj):
    j = pl.program_id(1)
    m = (_sigmoid(ml0_ref[...]) * ya_ref[...] + _sigmoid(ml1_ref[...]) * yf_ref[...]
         + _sigmoid(ml2_ref[...]) * yc_ref[...])
    part = jnp.dot(m.astype(BF16), w_ref[0], preferred_element_type=F32)

    @pl.when(j == 0)
    def _():
        acc_ref[...] = part

    @pl.when(j > 0)
    def _():
        acc_ref[...] += part

    @pl.when(j == nj - 1)
    def _():
        out = acc_ref[...]
        ms = jnp.mean(out * out, axis=-1, keepdims=True)
        r = out * lax.rsqrt(ms + EPS) * gp_ref[0]
        md = mod_ref[0]
        is_lat = _row_ids(pl.program_id(0), tm) < SEQ
        gate = jnp.where(is_lat, md[0:1, 2 * D:3 * D], md[1:2, 2 * D:3 * D])
        o_ref[...] = x_ref[...] + gate * r


def _final(rest, ya, yf, yc, w_out, x_all, mod, g_post, l):
    tm, tn = TM_S, 512
    nj = D // tn

    def ml(b):
        base = (OFF_ML + b * D) // tn
        return pl.BlockSpec((tm, tn), lambda i, j: (i, base + j))

    ysp = pl.BlockSpec((tm, tn), lambda i, j: (i, j))
    return pl.pallas_call(
        functools.partial(_final_kernel, tm=tm, nj=nj),
        out_shape=jax.ShapeDtypeStruct((S_ALL, D), F32),
        grid=(S_ALL // tm, nj),
        in_specs=[
            ml(0), ml(1), ml(2), ysp, ysp, ysp,
            pl.BlockSpec((1, tn, D), lambda i, j: (l, j, 0)),
            pl.BlockSpec((tm, D), lambda i, j: (i, 0)),
            pl.BlockSpec((1, 8, 3 * D), lambda i, j: (l, 0, 0)),
            pl.BlockSpec((1, 1, D), lambda i, j: (l, 0, 0)),
        ],
        out_specs=pl.BlockSpec((tm, D), lambda i, j: (i, 0)),
        scratch_shapes=[pltpu.VMEM((tm, D), F32)],
        compiler_params=_params("arbitrary", "arbitrary"),
        name="merge_out",
    )(rest, rest, rest, ya, yf, yc, w_out, x_all, mod, g_post)


def _rope_tables():
    half = HEAD_DIM // 4
    pos = jnp.arange(SEQ, dtype=jnp.int32)
    freqs = ROPE_THETA ** (-jnp.arange(half, dtype=F32) / half)
    ang_r = (pos // GRID_W).astype(F32)[:, None] * freqs[None, :]
    ang_c = (pos % GRID_W).astype(F32)[:, None] * freqs[None, :]
    cos = jnp.concatenate([jnp.cos(ang_r), jnp.cos(ang_r), jnp.cos(ang_c), jnp.cos(ang_c)], axis=-1)
    sin = jnp.concatenate([-jnp.sin(ang_r), jnp.sin(ang_r), -jnp.sin(ang_c), jnp.sin(ang_c)], axis=-1)
    cos = jnp.concatenate([cos, jnp.ones((CTX, HEAD_DIM), F32)], axis=0)
    sin = jnp.concatenate([sin, jnp.zeros((CTX, HEAD_DIM), F32)], axis=0)
    return cos, sin


def _dft_cos_sin(n):
    idx = jnp.arange(n, dtype=jnp.int32)
    ang = ((idx[:, None] * idx[None, :]) % n).astype(F32) * (2.0 * jnp.pi / n)
    return jnp.cos(ang), jnp.sin(ang)


def _dft_tables():
    c_c, s_c = _dft_cos_sin(F_GROUP_DIM)
    w_c = jnp.concatenate([c_c, -s_c], axis=1).astype(BF16)
    c_m, s_m = _dft_cos_sin(FFT_M)
    c_x, s_x = _dft_cos_sin(CTX)
    n1 = jnp.arange(FFT_R, dtype=jnp.int32)
    ang = ((n1[:, None] * jnp.arange(FFT_M, dtype=jnp.int32)[None, :]) % SEQ).astype(F32) * (2.0 * jnp.pi / SEQ)
    twc = jnp.broadcast_to(jnp.cos(ang)[:, :, None], (FFT_R, FFT_M, 128))
    tws = jnp.broadcast_to(jnp.sin(ang)[:, :, None], (FFT_R, FFT_M, 128))
    c_8, s_8 = _dft_cos_sin(FFT_R)
    w8 = jnp.stack([c_8, s_8])
    return dict(w_c=w_c, c_m=c_m.astype(BF16), s_m=s_m.astype(BF16), c_x=c_x.astype(BF16),
                s_x=s_x.astype(BF16), twc=twc, tws=tws, w8=w8)


def kernel(x, c, ctx, c_ctx, w_mod, b_mod, g_pre, g_post, w_in, q_norm, k_norm,
           w_attn_o, w_f_mix, w_f_o, conv_w, conv_b, w_conv_o, w_out):
    assert x.shape == (1, SEQ, D) and ctx.shape == (1, CTX, D) and w_in.shape == (DEPTH, D, IN_COLS)
    x_all = jnp.concatenate([x[0], ctx[0]], axis=0)
    c8 = jnp.concatenate([c, c_ctx[None, :], jnp.zeros((6, D), F32)], axis=0)
    cos_tab, sin_tab = _rope_tables()
    ft = _dft_tables()

    w_in_b = w_in.astype(BF16)
    w_attn_o_b = w_attn_o.astype(BF16)
    w_f_mix_b = w_f_mix.astype(BF16)
    w_f_o_b = w_f_o.astype(BF16)
    w_conv_o_b = w_conv_o.astype(BF16)
    w_out_b = w_out.astype(BF16)

    mod = _modulation(c8, w_mod.astype(BF16), b_mod.reshape(DEPTH, 1, 3 * D))
    g_pre3 = g_pre.reshape(DEPTH, 1, D)
    g_post3 = g_post.reshape(DEPTH, 1, D)
    conv_b3 = conv_b.reshape(DEPTH, 1, CONV_DIM)

    for l in range(DEPTH):
        nw = jnp.concatenate([jnp.tile(q_norm[l] * QK_PRESCALE, N_HEADS), jnp.tile(k_norm[l], N_KV),
                              jnp.ones((KV_DIM,), F32)])[None, :]
        h = _prenorm(x_all, mod, g_pre3, l)
        qkv = _inproj_qkv(h, w_in_b, nw, cos_tab, sin_tab, l)
        rest = _inproj_rest(h, w_in_b, l)
        attn = _attention_dispatch(qkv, q_norm[l], k_norm[l])
        zr, zi = _fourier_lat_a(rest, ft["w_c"])
        fmix_lat = _fourier_lat_b(ft["w8"], ft["c_m"], ft["s_m"], zr, zi, ft["twc"], ft["tws"])
        fmix_ctx = _fourier_ctx(rest, ft["w_c"], ft["c_x"], ft["s_x"])
        fmix = jnp.concatenate([fmix_lat, fmix_ctx], axis=0)
        ya = _proj_a(attn, rest, w_attn_o_b, l)
        yf = _proj_f(fmix, rest, w_f_mix_b, w_f_o_b, l)
        yc = _proj_c(rest, conv_w, conv_b3, w_conv_o_b, l)
        x_all = _final(rest, ya, yf, yc, w_out_b, x_all, mod, g_post3, l)

    return x_all[:SEQ][None]
```

```python
import functools

import jax
import jax.numpy as jnp
from jax import lax
from jax.experimental import pallas as pl
from jax.experimental.pallas import tpu as pltpu

D = 2048
SEQ = 8192
CTX = 256
S_ALL = SEQ + CTX
DEPTH = 4
GRID_W = 64
HEAD_DIM = 128
N_HEADS = 16
N_KV = 4
Q_PER_KV = N_HEADS // N_KV
KV_DIM = N_KV * HEAD_DIM
F_GROUPS = 4
F_DIM = 1024
F_GROUP_DIM = F_DIM // F_GROUPS
CONV_DIM = 1024
ROPE_THETA = 10000.0
EPS = 1e-6
QK_PRESCALE = 1.4426950408889634 / float(HEAD_DIM) ** 0.5
IN_COLS = 17408
QKV_COLS = D + 2 * KV_DIM
REST_COLS = IN_COLS - QKV_COLS
OFF_AG, OFF_FX, OFF_FG, OFF_CX, OFF_CB, OFF_CC, OFF_CG, OFF_ML = (
    0, 2048, 3072, 4096, 5120, 6144, 7168, 8192)

TM = 1056
TM_W = 2112
TM_S = 528
VMEM_LIMIT_BYTES = 56 * 1024 * 1024

F32 = jnp.float32
BF16 = jnp.bfloat16


def _params(*sem):
    return pltpu.CompilerParams(dimension_semantics=sem, vmem_limit_bytes=VMEM_LIMIT_BYTES)


def _sigmoid(v):
    return 1.0 / (1.0 + jnp.exp(-v))


def _silu(v):
    return v * _sigmoid(v)


def _lane_tile(v, rep):
    return v if rep == 1 else jnp.concatenate([v] * rep, axis=1)


def _row_ids(i, tm):
    return i * tm + lax.broadcasted_iota(jnp.int32, (tm, 1), 0)


def _mod_kernel(c_ref, w_ref, b_ref, o_ref):
    sc = _silu(c_ref[...])
    o_ref[0] = jnp.dot(sc.astype(BF16), w_ref[0].astype(BF16), preferred_element_type=F32) + b_ref[0]


def _modulation(c8, w_mod, b_mod):
    tn = 1024
    return pl.pallas_call(
        _mod_kernel,
        out_shape=jax.ShapeDtypeStruct((DEPTH, 8, 3 * D), F32),
        grid=(DEPTH, 3 * D // tn),
        in_specs=[
            pl.BlockSpec((8, D), lambda l, j: (0, 0)),
            pl.BlockSpec((1, D, tn), lambda l, j: (l, 0, j)),
            pl.BlockSpec((1, 1, tn), lambda l, j: (l, 0, j)),
        ],
        out_specs=pl.BlockSpec((1, 8, tn), lambda l, j: (l, 0, j)),
        compiler_params=_params("arbitrary", "arbitrary"),
        name="modulation",
    )(c8, w_mod, b_mod)


def _prenorm_kernel(x_ref, mod_ref, g_ref, h_ref, *, tm):
    x = x_ref[...]
    ms = jnp.mean(x * x, axis=-1, keepdims=True)
    y = x * lax.rsqrt(ms + EPS) * g_ref[0]
    m = mod_ref[0]
    is_lat = _row_ids(pl.program_id(0), tm) < SEQ
    shift = jnp.where(is_lat, m[0:1, 0:D], m[1:2, 0:D])
    scale = jnp.where(is_lat, m[0:1, D:2 * D], m[1:2, D:2 * D])
    h_ref[...] = (y * (1.0 + scale) + shift).astype(BF16)


def _prenorm(x_all, mod, g_pre, l):
    tm = TM_S
    return pl.pallas_call(
        functools.partial(_prenorm_kernel, tm=tm),
        out_shape=jax.ShapeDtypeStruct((S_ALL, D), BF16),
        grid=(S_ALL // tm,),
        in_specs=[
            pl.BlockSpec((tm, D), lambda i: (i, 0)),
            pl.BlockSpec((1, 8, 3 * D), lambda i: (l, 0, 0)),
            pl.BlockSpec((1, 1, D), lambda i: (l, 0, 0)),
        ],
        out_specs=pl.BlockSpec((tm, D), lambda i: (i, 0)),
        compiler_params=_params("arbitrary"),
        name="prenorm",
    )(x_all, mod, g_pre)


def _swap_halves(t):
    lane = lax.broadcasted_iota(jnp.int32, t.shape, 1)
    return jnp.where((lane % 64) < 32, pltpu.roll(t, 96, 1), pltpu.roll(t, 32, 1))


def _inproj_qk_kernel(h_ref, w_ref, nw_ref, cos_ref, sin_ref, o_ref, *, tn):
    acc = jnp.dot(h_ref[...], w_ref[0], preferred_element_type=F32)
    cos = cos_ref[...]
    sin = sin_ref[...]
    for hh in range(tn // HEAD_DIM):
        sl = slice(hh * HEAD_DIM, (hh + 1) * HEAD_DIM)
        t = acc[:, sl]
        ms = jnp.mean(t * t, axis=-1, keepdims=True)
        y = t * lax.rsqrt(ms + EPS) * nw_ref[:, sl]
        o_ref[:, sl] = (y * cos + _swap_halves(y) * sin).astype(BF16)


def _inproj_qk(h, w_qk, nw, cos_tab, sin_tab, l):
    tn = 512
    return pl.pallas_call(
        functools.partial(_inproj_qk_kernel, tn=tn),
        out_shape=jax.ShapeDtypeStruct((S_ALL, D + KV_DIM), BF16),
        grid=(S_ALL // TM, (D + KV_DIM) // tn),
        in_specs=[
            pl.BlockSpec((TM, D), lambda i, j: (i, 0)),
            pl.BlockSpec((1, D, tn), lambda i, j: (l, 0, j)),
            pl.BlockSpec((1, tn), lambda i, j: (0, j)),
            pl.BlockSpec((TM, HEAD_DIM), lambda i, j: (i, 0)),
            pl.BlockSpec((TM, HEAD_DIM), lambda i, j: (i, 0)),
        ],
        out_specs=pl.BlockSpec((TM, tn), lambda i, j: (i, j)),
        compiler_params=_params("arbitrary", "arbitrary"),
        name="inproj_qk",
    )(h, w_qk, nw, cos_tab, sin_tab)


def _inproj_plain_kernel(h_ref, w_ref, o_ref):
    o_ref[...] = jnp.dot(h_ref[...], w_ref[0].astype(BF16),
                         preferred_element_type=F32).astype(o_ref.dtype)


def _inproj_plain(h, w_in, l, col0, ncols, out_dtype, name):
    tn = 512
    off = col0 // tn
    return pl.pallas_call(
        _inproj_plain_kernel,
        out_shape=jax.ShapeDtypeStruct((S_ALL, ncols), out_dtype),
        grid=(S_ALL // TM_W, ncols // tn),
        in_specs=[
            pl.BlockSpec((TM_W, D), lambda i, j: (i, 0)),
            pl.BlockSpec((1, D, tn), lambda i, j: (l, 0, off + j)),
        ],
        out_specs=pl.BlockSpec((TM_W, tn), lambda i, j: (i, j)),
        compiler_params=_params("arbitrary", "arbitrary"),
        name=name,
    )(h, w_in)


def _attn_kernel(bound_ref, q_ref, k_ref, v_ref, o_ref, q4_ref, m_ref, l_ref, acc_ref,
                 *, tq, tk, groups, bounded):
    qi = pl.program_id(1)
    for hh in range(Q_PER_KV):
        q4_ref[hh * tq:(hh + 1) * tq, :] = q_ref[:, hh * HEAD_DIM:(hh + 1) * HEAD_DIM]
    if not bounded:
        m_ref[...] = jnp.full(m_ref.shape, -jnp.inf, F32)
    l_ref[...] = jnp.zeros(l_ref.shape, F32)
    acc_ref[...] = jnp.zeros(acc_ref.shape, F32)
    rows = Q_PER_KV * tq // groups

    def step(start, size):
        k = k_ref[pl.ds(start, size), :]
        v = v_ref[pl.ds(start, size), :]
        rep = size // HEAD_DIM
        scores = [lax.dot_general(q4_ref[r * rows:(r + 1) * rows, :], k, (((1,), (1,)), ((), ())),
                                  preferred_element_type=F32) for r in range(groups)]
        for r in range(groups):
            rs = slice(r * rows, (r + 1) * rows)
            s = scores[r]
            if bounded:
                p = jnp.exp2(s - bound_ref[0])
                lp = p[:, 0:HEAD_DIM]
                for t in range(1, rep):
                    lp = lp + p[:, t * HEAD_DIM:(t + 1) * HEAD_DIM]
                l_ref[rs, :] += lp
                acc_ref[rs, :] += jnp.dot(p.astype(BF16), v, preferred_element_type=F32)
            else:
                m_prev = m_ref[rs, :]
                m_new = jnp.maximum(m_prev, jnp.max(s, axis=-1, keepdims=True))
                alpha = jnp.exp2(m_prev - m_new)
                p = jnp.exp2(s - _lane_tile(m_new, rep))
                l_ref[rs, :] = alpha * l_ref[rs, :] + jnp.sum(p, axis=-1, keepdims=True)
                acc_ref[rs, :] = alpha * acc_ref[rs, :] + jnp.dot(p.astype(BF16), v,
                                                                  preferred_element_type=F32)
                m_ref[rs, :] = m_new

    @pl.when(qi < SEQ // tq)
    def _():
        def body(c, carry):
            step(pl.multiple_of(c * tk, tk), tk)
            return carry
        lax.fori_loop(0, SEQ // tk, body, 0)

    step(SEQ, CTX)
    l = jnp.sum(l_ref[...], axis=-1, keepdims=True) if bounded else l_ref[...]
    o = acc_ref[...] / l
    for hh in range(Q_PER_KV):
        o_ref[:, hh * HEAD_DIM:(hh + 1) * HEAD_DIM] = o[hh * tq:(hh + 1) * tq, :]


def _attention(qk, v, bound, bounded):
    tq, tk, groups = 256, 4096, 4
    gw = Q_PER_KV * HEAD_DIM
    k_blk = D // HEAD_DIM
    return pl.pallas_call(
        functools.partial(_attn_kernel, tq=tq, tk=tk, groups=groups, bounded=bounded),
        out_shape=jax.ShapeDtypeStruct((S_ALL, D), F32),
        grid=(N_KV, S_ALL // tq),
        in_specs=[
            pl.BlockSpec(memory_space=pltpu.SMEM),
            pl.BlockSpec((tq, gw), lambda g, qi: (qi, g)),
            pl.BlockSpec((S_ALL, HEAD_DIM), lambda g, qi: (0, k_blk + g)),
            pl.BlockSpec((S_ALL, HEAD_DIM), lambda g, qi: (0, g)),
        ],
        out_specs=pl.BlockSpec((tq, gw), lambda g, qi: (qi, g)),
        scratch_shapes=[
            pltpu.VMEM((Q_PER_KV * tq, HEAD_DIM), BF16),
            pltpu.VMEM((Q_PER_KV * tq, HEAD_DIM), F32),
            pltpu.VMEM((Q_PER_KV * tq, HEAD_DIM), F32),
            pltpu.VMEM((Q_PER_KV * tq, HEAD_DIM), F32),
        ],
        compiler_params=_params("arbitrary", "arbitrary"),
        name="attention_bounded" if bounded else "attention_online",
    )(bound, qk, qk, v)


MAX_SCORE_BOUND = 60.0


def _attention_dispatch(qk, v, qn, kn):
    bound = (1.02 * HEAD_DIM * QK_PRESCALE) * jnp.max(jnp.abs(qn)) * jnp.max(jnp.abs(kn))
    b1 = bound.reshape(1).astype(F32)
    return lax.cond(bound <= MAX_SCORE_BOUND,
                    lambda: _attention(qk, v, b1, True),
                    lambda: _attention(qk, v, b1, False))


FFT_R = 8
FFT_M = SEQ // FFT_R


def _four_lat_a_kernel(u_ref, w_ref, zr_ref, zi_ref, r_ref, *, tm):
    r = jnp.dot(u_ref[...].astype(BF16), w_ref[...], preferred_element_type=F32)
    nt = F_GROUP_DIM // 128
    for t in range(2 * nt):
        r_ref[t] = r[:, t * 128:(t + 1) * 128]
    for n1 in range(FFT_R):
        for t in range(2 * nt):
            blk = r_ref[t, pl.ds(n1, tm // FFT_R, stride=FFT_R), :]
            dst = zr_ref if t < nt else zi_ref
            tt = t % nt
            dst[n1, :, tt * 128:(tt + 1) * 128] = blk.astype(BF16)


def _fourier_lat_a(rest, w_c):
    tm = 1024
    off = OFF_FX // F_GROUP_DIM
    out = jax.ShapeDtypeStruct((FFT_R, FFT_M, F_DIM), BF16)
    zspec = pl.BlockSpec((FFT_R, tm // FFT_R, F_GROUP_DIM), lambda i, g: (0, i, g))
    return pl.pallas_call(
        functools.partial(_four_lat_a_kernel, tm=tm),
        out_shape=(out, out),
        grid=(SEQ // tm, F_GROUPS),
        in_specs=[
            pl.BlockSpec((tm, F_GROUP_DIM), lambda i, g: (i, off + g)),
            pl.BlockSpec((F_GROUP_DIM, 2 * F_GROUP_DIM), lambda i, g: (0, 0)),
        ],
        out_specs=(zspec, zspec),
        scratch_shapes=[pltpu.VMEM((2 * F_GROUP_DIM // 128, tm, 128), F32)],
        compiler_params=_params("arbitrary", "arbitrary"),
        name="fourier_lat_a",
    )(rest, w_c)


def _four_lat_b_kernel(w8_ref, c_ref, s_ref, zr_ref, zi_ref, twc_ref, tws_ref, o_ref, acc_ref, *, tc):
    n1 = pl.program_id(1)

    @pl.when(n1 == 0)
    def _():
        acc_ref[...] = jnp.zeros(acc_ref.shape, F32)

    zr = zr_ref[0]
    zi = zi_ref[0]
    cm = c_ref[...]
    sm = s_ref[...]
    vr = (jnp.dot(cm, zr, preferred_element_type=F32) + jnp.dot(sm, zi, preferred_element_type=F32))
    vi = (jnp.dot(cm, zi, preferred_element_type=F32) - jnp.dot(sm, zr, preferred_element_type=F32))
    twc = _lane_tile(twc_ref[0], tc // 128)
    tws = _lane_tile(tws_ref[0], tc // 128)
    pr = vr * twc + vi * tws
    pi = vi * twc - vr * tws
    for k1 in range(FFT_R):
        acc_ref[k1] += w8_ref[0, n1, k1] * pr + w8_ref[1, n1, k1] * pi

    @pl.when(n1 == FFT_R - 1)
    def _():
        scale = 1.0 / float(SEQ * F_GROUP_DIM) ** 0.5
        for k1 in range(FFT_R):
            o_ref[k1 * FFT_M:(k1 + 1) * FFT_M, :] = (acc_ref[k1] * scale).astype(BF16)


def _fourier_lat_b(w8, c_m, s_m, zr, zi, twc, tws):
    tc = 256
    zspec = pl.BlockSpec((1, FFT_M, tc), lambda j, n1: (n1, 0, j))
    twspec = pl.BlockSpec((1, FFT_M, 128), lambda j, n1: (n1, 0, 0))
    mspec = pl.BlockSpec((FFT_M, FFT_M), lambda j, n1: (0, 0))
    return pl.pallas_call(
        functools.partial(_four_lat_b_kernel, tc=tc),
        out_shape=jax.ShapeDtypeStruct((SEQ, F_DIM), BF16),
        grid=(F_DIM // tc, FFT_R),
        in_specs=[pl.BlockSpec(memory_space=pltpu.SMEM), mspec, mspec, zspec, zspec, twspec, twspec],
        out_specs=pl.BlockSpec((SEQ, tc), lambda j, n1: (0, j)),
        scratch_shapes=[pltpu.VMEM((FFT_R, FFT_M, tc), F32)],
        compiler_params=_params("arbitrary", "arbitrary"),
        name="fourier_lat_b",
    )(w8, c_m, s_m, zr, zi, twc, tws)


def _four_ctx_kernel(u_ref, w_ref, c_ref, s_ref, o_ref):
    z = jnp.dot(u_ref[...].astype(BF16), w_ref[...], preferred_element_type=F32)
    zr = z[:, :F_GROUP_DIM].astype(BF16)
    zi = z[:, F_GROUP_DIM:].astype(BF16)
    y = (jnp.dot(c_ref[...], zr, preferred_element_type=F32)
         + jnp.dot(s_ref[...], zi, preferred_element_type=F32))
    o_ref[...] = (y * (1.0 / float(CTX * F_GROUP_DIM) ** 0.5)).astype(BF16)


def _fourier_ctx(rest, w_c, c_x, s_x):
    off = OFF_FX // F_GROUP_DIM
    mspec = pl.BlockSpec((CTX, CTX), lambda g: (0, 0))
    return pl.pallas_call(
        _four_ctx_kernel,
        out_shape=jax.ShapeDtypeStruct((CTX, F_DIM), BF16),
        grid=(F_GROUPS,),
        in_specs=[
            pl.BlockSpec((CTX, F_GROUP_DIM), lambda g: (SEQ // CTX, off + g)),
            pl.BlockSpec((F_GROUP_DIM, 2 * F_GROUP_DIM), lambda g: (0, 0)),
            mspec, mspec,
        ],
        out_specs=pl.BlockSpec((CTX, F_GROUP_DIM), lambda g: (0, g)),
        compiler_params=_params("arbitrary"),
        name="fourier_ctx",
    )(rest, w_c, c_x, s_x)


def _proj_a_kernel(attn_ref, ag_ref, w_ref, o_ref, a_ref):
    @pl.when(pl.program_id(1) == 0)
    def _():
        a_ref[...] = (attn_ref[...] * _silu(ag_ref[...])).astype(BF16)

    o_ref[...] = jnp.dot(a_ref[...], w_ref[0], preferred_element_type=F32)


def _proj_a(attn, rest, w_attn_o, l):
    tm, tn = TM_S, 1024
    return pl.pallas_call(
        _proj_a_kernel,
        out_shape=jax.ShapeDtypeStruct((S_ALL, D), F32),
        grid=(S_ALL // tm, D // tn),
        in_specs=[
            pl.BlockSpec((tm, D), lambda i, j: (i, 0)),
            pl.BlockSpec((tm, D), lambda i, j: (i, OFF_AG // D)),
            pl.BlockSpec((1, D, tn), lambda i, j: (l, 0, j)),
        ],
        out_specs=pl.BlockSpec((tm, tn), lambda i, j: (i, j)),
        scratch_shapes=[pltpu.VMEM((tm, D), BF16)],
        compiler_params=_params("arbitrary", "arbitrary"),
        name="proj_attn",
    )(attn, rest, w_attn_o)


def _proj_f_kernel(y_ref, fg_ref, wmix_ref, wo_ref, o_ref):
    t = jnp.dot(y_ref[...], wmix_ref[0], preferred_element_type=F32) * _silu(fg_ref[...])
    o_ref[...] = jnp.dot(t.astype(BF16), wo_ref[0], preferred_element_type=F32)


def _proj_f(fmix, rest, w_f_mix, w_f_o, l):
    tm = TM_S
    return pl.pallas_call(
        _proj_f_kernel,
        out_shape=jax.ShapeDtypeStruct((S_ALL, D), F32),
        grid=(S_ALL // tm,),
        in_specs=[
            pl.BlockSpec((tm, F_DIM), lambda i: (i, 0)),
            pl.BlockSpec((tm, F_DIM), lambda i: (i, OFF_FG // F_DIM)),
            pl.BlockSpec((1, F_DIM, F_DIM), lambda i: (l, 0, 0)),
            pl.BlockSpec((1, F_DIM, D), lambda i: (l, 0, 0)),
        ],
        out_specs=pl.BlockSpec((tm, D), lambda i: (i, 0)),
        compiler_params=_params("arbitrary"),
        name="proj_fourier",
    )(fmix, rest, w_f_mix, w_f_o)


def _proj_c_kernel(cx_ref, cb_ref, cc_ref, cg_ref, cxp_ref, ccp_ref, cxn_ref, ccn_ref,
                   cw_ref, cbias_ref, w_ref, o_ref, *, tm):
    u = cc_ref[...] * cx_ref[...]
    u_before = ccp_ref[7:8, :] * cxp_ref[7:8, :]
    u_after = ccn_ref[0:1, :] * cxn_ref[0:1, :]
    rows = _row_ids(pl.program_id(0), tm)
    loc = lax.broadcasted_iota(jnp.int32, (tm, 1), 0)
    u_prev = jnp.where(loc == 0, u_before, pltpu.roll(u, 1, 0))
    u_prev = jnp.where((rows == 0) | (rows == SEQ), 0.0, u_prev)
    u_next = jnp.where(loc == tm - 1, u_after, pltpu.roll(u, tm - 1, 0))
    u_next = jnp.where((rows == SEQ - 1) | (rows == S_ALL - 1), 0.0, u_next)
    cw = cw_ref[0]
    conv = u_prev * cw[0:1, :] + u * cw[1:2, :] + u_next * cw[2:3, :] + cbias_ref[0]
    t = cb_ref[...] * conv * _silu(cg_ref[...])
    o_ref[...] = jnp.dot(t.astype(BF16), w_ref[0], preferred_element_type=F32)


def _proj_c(rest, conv_w, conv_b, w_conv_o, l):
    tm = TM_S
    rb = tm // 8
    last = S_ALL // 8 - 1

    def col(off):
        return pl.BlockSpec((tm, CONV_DIM), lambda i: (i, off // CONV_DIM))

    def above(off):
        return pl.BlockSpec((8, CONV_DIM), lambda i: (jnp.maximum(i * rb - 1, 0), off // CONV_DIM))

    def below(off):
        return pl.BlockSpec((8, CONV_DIM), lambda i: (jnp.minimum((i + 1) * rb, last), off // CONV_DIM))

    return pl.pallas_call(
        functools.partial(_proj_c_kernel, tm=tm),
        out_shape=jax.ShapeDtypeStruct((S_ALL, D), F32),
        grid=(S_ALL // tm,),
        in_specs=[
            col(OFF_CX), col(OFF_CB), col(OFF_CC), col(OFF_CG),
            above(OFF_CX), above(OFF_CC), below(OFF_CX), below(OFF_CC),
            pl.BlockSpec((1, 3, CONV_DIM), lambda i: (l, 0, 0)),
            pl.BlockSpec((1, 1, CONV_DIM), lambda i: (l, 0, 0)),
            pl.BlockSpec((1, CONV_DIM, D), lambda i: (l, 0, 0)),
        ],
        out_specs=pl.BlockSpec((tm, D), lambda i: (i, 0)),
        compiler_params=_params("arbitrary"),
        name="proj_conv",
    )(rest, rest, rest, rest, rest, rest, rest, rest, conv_w, conv_b, w_conv_o)


def _final_kernel(ml0_ref, ml1_ref, ml2_ref, ya_ref, yf_ref, yc_ref, w_ref, x_ref,
                  mod_ref, gp_ref, o_ref, acc_ref, *, tm, nj):
    j = pl.program_id(1)
    m = (_sigmoid(ml0_ref[...]) * ya_ref[...] + _sigmoid(ml1_ref[...]) * yf_ref[...]
         + _sigmoid(ml2_ref[...]) * yc_ref[...])
    part = jnp.dot(m.astype(BF16), w_ref[0], preferred_element_type=F32)

    @pl.when(j == 0)
    def _():
        acc_ref[...] = part

    @pl.when(j > 0)
    def _():
        acc_ref[...] += part

    @pl.when(j == nj - 1)
    def _():
        out = acc_ref[...]
        ms = jnp.mean(out * out, axis=-1, keepdims=True)
        r = out * lax.rsqrt(ms + EPS) * gp_ref[0]
        md = mod_ref[0]
        is_lat = _row_ids(pl.program_id(0), tm) < SEQ
        gate = jnp.where(is_lat, md[0:1, 2 * D:3 * D], md[1:2, 2 * D:3 * D])
        o_ref[...] = x_ref[...] + gate * r


def _final(rest, ya, yf, yc, w_out, x_all, mod, g_post, l):
    tm, tn = TM_S, 512
    nj = D // tn

    def ml(b):
        base = (OFF_ML + b * D) // tn
        return pl.BlockSpec((tm, tn), lambda i, j: (i, base + j))

    ysp = pl.BlockSpec((tm, tn), lambda i, j: (i, j))
    return pl.pallas_call(
        functools.partial(_final_kernel, tm=tm, nj=nj),
        out_shape=jax.ShapeDtypeStruct((S_ALL, D), F32),
        grid=(S_ALL // tm, nj),
        in_specs=[
            ml(0), ml(1), ml(2), ysp, ysp, ysp,
            pl.BlockSpec((1, tn, D), lambda i, j: (l, j, 0)),
            pl.BlockSpec((tm, D), lambda i, j: (i, 0)),
            pl.BlockSpec((1, 8, 3 * D), lambda i, j: (l, 0, 0)),
            pl.BlockSpec((1, 1, D), lambda i, j: (l, 0, 0)),
        ],
        out_specs=pl.BlockSpec((tm, D), lambda i, j: (i, 0)),
        scratch_shapes=[pltpu.VMEM((tm, D), F32)],
        compiler_params=_params("arbitrary", "arbitrary"),
        name="merge_out",
    )(rest, rest, rest, ya, yf, yc, w_out, x_all, mod, g_post)


def _rope_tables():
    half = HEAD_DIM // 4
    pos = jnp.arange(SEQ, dtype=jnp.int32)
    freqs = ROPE_THETA ** (-jnp.arange(half, dtype=F32) / half)
    ang_r = (pos // GRID_W).astype(F32)[:, None] * freqs[None, :]
    ang_c = (pos % GRID_W).astype(F32)[:, None] * freqs[None, :]
    cos = jnp.concatenate([jnp.cos(ang_r), jnp.cos(ang_r), jnp.cos(ang_c), jnp.cos(ang_c)], axis=-1)
    sin = jnp.concatenate([-jnp.sin(ang_r), jnp.sin(ang_r), -jnp.sin(ang_c), jnp.sin(ang_c)], axis=-1)
    cos = jnp.concatenate([cos, jnp.ones((CTX, HEAD_DIM), F32)], axis=0)
    sin = jnp.concatenate([sin, jnp.zeros((CTX, HEAD_DIM), F32)], axis=0)
    return cos, sin


def _dft_cos_sin(n):
    idx = jnp.arange(n, dtype=jnp.int32)
    ang = ((idx[:, None] * idx[None, :]) % n).astype(F32) * (2.0 * jnp.pi / n)
    return jnp.cos(ang), jnp.sin(ang)


def _dft_tables():
    c_c, s_c = _dft_cos_sin(F_GROUP_DIM)
    w_c = jnp.concatenate([c_c, -s_c], axis=1).astype(BF16)
    c_m, s_m = _dft_cos_sin(FFT_M)
    c_x, s_x = _dft_cos_sin(CTX)
    n1 = jnp.arange(FFT_R, dtype=jnp.int32)
    ang = ((n1[:, None] * jnp.arange(FFT_M, dtype=jnp.int32)[None, :]) % SEQ).astype(F32) * (2.0 * jnp.pi / SEQ)
    twc = jnp.broadcast_to(jnp.cos(ang)[:, :, None], (FFT_R, FFT_M, 128))
    tws = jnp.broadcast_to(jnp.sin(ang)[:, :, None], (FFT_R, FFT_M, 128))
    c_8, s_8 = _dft_cos_sin(FFT_R)
    w8 = jnp.stack([c_8, s_8])
    return dict(w_c=w_c, c_m=c_m.astype(BF16), s_m=s_m.astype(BF16), c_x=c_x.astype(BF16),
                s_x=s_x.astype(BF16), twc=twc, tws=tws, w8=w8)


def kernel(x, c, ctx, c_ctx, w_mod, b_mod, g_pre, g_post, w_in, q_norm, k_norm,
           w_attn_o, w_f_mix, w_f_o, conv_w, conv_b, w_conv_o, w_out):
    assert x.shape == (1, SEQ, D) and ctx.shape == (1, CTX, D) and w_in.shape == (DEPTH, D, IN_COLS)
    x_all = jnp.concatenate([x[0], ctx[0]], axis=0)
    c8 = jnp.concatenate([c, c_ctx[None, :], jnp.zeros((6, D), F32)], axis=0)
    cos_tab, sin_tab = _rope_tables()
    ft = _dft_tables()

    w_qk_b = w_in[:, :, :D + KV_DIM].astype(BF16)
    w_attn_o_b = w_attn_o.astype(BF16)
    w_f_mix_b = w_f_mix.astype(BF16)
    w_f_o_b = w_f_o.astype(BF16)
    w_conv_o_b = w_conv_o.astype(BF16)
    w_out_b = w_out.astype(BF16)

    mod = _modulation(c8, w_mod, b_mod.reshape(DEPTH, 1, 3 * D))
    g_pre3 = g_pre.reshape(DEPTH, 1, D)
    g_post3 = g_post.reshape(DEPTH, 1, D)
    conv_b3 = conv_b.reshape(DEPTH, 1, CONV_DIM)

    for l in range(DEPTH):
        nw = jnp.concatenate([jnp.tile(q_norm[l] * QK_PRESCALE, N_HEADS),
                              jnp.tile(k_norm[l], N_KV)])[None, :]
        h = _prenorm(x_all, mod, g_pre3, l)
        qk = _inproj_qk(h, w_qk_b, nw, cos_tab, sin_tab, l)
        v = _inproj_plain(h, w_in, l, D + KV_DIM, KV_DIM, BF16, "inproj_v")
        rest = _inproj_plain(h, w_in, l, QKV_COLS, REST_COLS, F32, "inproj_rest")
        attn = _attention_dispatch(qk, v, q_norm[l], k_norm[l])
        zr, zi = _fourier_lat_a(rest, ft["w_c"])
        fmix_lat = _fourier_lat_b(ft["w8"], ft["c_m"], ft["s_m"], zr, zi, ft["twc"], ft["tws"])
        fmix_ctx = _fourier_ctx(rest, ft["w_c"], ft["c_x"], ft["s_x"])
        fmix = jnp.concatenate([fmix_lat, fmix_ctx], axis=0)
        ya = _proj_a(attn, rest, w_attn_o_b, l)
        yf = _proj_f(fmix, rest, w_f_mix_b, w_f_o_b, l)
        yc = _proj_c(rest, conv_w, conv_b3, w_conv_o_b, l)
        x_all = _final(rest, ya, yf, yc, w_out_b, x_all, mod, g_post3, l)

    return x_all[:SEQ][None]
```

```python
import functools

import jax
import jax.numpy as jnp
from jax import lax
from jax.experimental import pallas as pl
from jax.experimental.pallas import tpu as pltpu

D = 2048
SEQ = 8192
CTX = 256
S_ALL = SEQ + CTX
DEPTH = 4
GRID_W = 64
HEAD_DIM = 128
N_HEADS = 16
N_KV = 4
Q_PER_KV = N_HEADS // N_KV
KV_DIM = N_KV * HEAD_DIM
F_GROUPS = 4
F_DIM = 1024
F_GROUP_DIM = F_DIM // F_GROUPS
CONV_DIM = 1024
ROPE_THETA = 10000.0
EPS = 1e-6
QK_PRESCALE = 1.4426950408889634 / float(HEAD_DIM) ** 0.5
IN_COLS = 17408
QKV_COLS = D + 2 * KV_DIM
REST_COLS = IN_COLS - QKV_COLS
OFF_AG, OFF_FX, OFF_FG, OFF_CX, OFF_CB, OFF_CC, OFF_CG, OFF_ML = (
    0, 2048, 3072, 4096, 5120, 6144, 7168, 8192)

TM = 1056
TM_W = 2112
TM_S = 528
VMEM_LIMIT_BYTES = 56 * 1024 * 1024

F32 = jnp.float32
BF16 = jnp.bfloat16


def _params(*sem):
    return pltpu.CompilerParams(dimension_semantics=sem, vmem_limit_bytes=VMEM_LIMIT_BYTES)


def _sigmoid(v):
    return 1.0 / (1.0 + jnp.exp(-v))


def _silu(v):
    return v * _sigmoid(v)


def _lane_tile(v, rep):
    return v if rep == 1 else jnp.concatenate([v] * rep, axis=1)


def _row_ids(i, tm):
    return i * tm + lax.broadcasted_iota(jnp.int32, (tm, 1), 0)


def _mod_kernel(c_ref, w_ref, b_ref, o_ref):
    sc = _silu(c_ref[...])
    o_ref[0] = jnp.dot(sc.astype(BF16), w_ref[0].astype(BF16), preferred_element_type=F32) + b_ref[0]


def _modulation(c8, w_mod, b_mod):
    tn = 1024
    return pl.pallas_call(
        _mod_kernel,
        out_shape=jax.ShapeDtypeStruct((DEPTH, 8, 3 * D), F32),
        grid=(DEPTH, 3 * D // tn),
        in_specs=[
            pl.BlockSpec((8, D), lambda l, j: (0, 0)),
            pl.BlockSpec((1, D, tn), lambda l, j: (l, 0, j)),
            pl.BlockSpec((1, 1, tn), lambda l, j: (l, 0, j)),
        ],
        out_specs=pl.BlockSpec((1, 8, tn), lambda l, j: (l, 0, j)),
        compiler_params=_params("arbitrary", "arbitrary"),
        name="modulation",
    )(c8, w_mod, b_mod)


def _prenorm_kernel(x_ref, mod_ref, g_ref, h_ref, *, tm):
    x = x_ref[...]
    ms = jnp.mean(x * x, axis=-1, keepdims=True)
    y = x * lax.rsqrt(ms + EPS) * g_ref[0]
    m = mod_ref[0]
    is_lat = _row_ids(pl.program_id(0), tm) < SEQ
    shift = jnp.where(is_lat, m[0:1, 0:D], m[1:2, 0:D])
    scale = jnp.where(is_lat, m[0:1, D:2 * D], m[1:2, D:2 * D])
    h_ref[...] = (y * (1.0 + scale) + shift).astype(BF16)


def _prenorm(x_all, mod, g_pre, l):
    tm = TM_S
    return pl.pallas_call(
        functools.partial(_prenorm_kernel, tm=tm),
        out_shape=jax.ShapeDtypeStruct((S_ALL, D), BF16),
        grid=(S_ALL // tm,),
        in_specs=[
            pl.BlockSpec((tm, D), lambda i: (i, 0)),
            pl.BlockSpec((1, 8, 3 * D), lambda i: (l, 0, 0)),
            pl.BlockSpec((1, 1, D), lambda i: (l, 0, 0)),
        ],
        out_specs=pl.BlockSpec((tm, D), lambda i: (i, 0)),
        compiler_params=_params("arbitrary"),
        name="prenorm",
    )(x_all, mod, g_pre)


def _swap_halves(t):
    lane = lax.broadcasted_iota(jnp.int32, t.shape, 1)
    return jnp.where((lane % 64) < 32, pltpu.roll(t, 96, 1), pltpu.roll(t, 32, 1))


def _inproj_qk_kernel(h_ref, w_ref, nw_ref, cos_ref, sin_ref, o_ref, *, tn):
    acc = jnp.dot(h_ref[...], w_ref[0], preferred_element_type=F32)

    @pl.when(pl.program_id(1) >= 0)
    def _():
        cos = cos_ref[...]
        sin = sin_ref[...]
        for hh in range(tn // HEAD_DIM):
            sl = slice(hh * HEAD_DIM, (hh + 1) * HEAD_DIM)
            t = acc[:, sl]
            ms = jnp.mean(t * t, axis=-1, keepdims=True)
            y = t * lax.rsqrt(ms + EPS) * nw_ref[:, sl]
            o_ref[:, sl] = (y * cos + _swap_halves(y) * sin).astype(BF16)


def _inproj_qk(h, w_qk, nw, cos_tab, sin_tab, l):
    tn = 512
    return pl.pallas_call(
        functools.partial(_inproj_qk_kernel, tn=tn),
        out_shape=jax.ShapeDtypeStruct((S_ALL, D + KV_DIM), BF16),
        grid=(S_ALL // TM, (D + KV_DIM) // tn),
        in_specs=[
            pl.BlockSpec((TM, D), lambda i, j: (i, 0)),
            pl.BlockSpec((1, D, tn), lambda i, j: (l, 0, j)),
            pl.BlockSpec((1, tn), lambda i, j: (0, j)),
            pl.BlockSpec((TM, HEAD_DIM), lambda i, j: (i, 0)),
            pl.BlockSpec((TM, HEAD_DIM), lambda i, j: (i, 0)),
        ],
        out_specs=pl.BlockSpec((TM, tn), lambda i, j: (i, j)),
        compiler_params=_params("arbitrary", "arbitrary"),
        name="inproj_qk",
    )(h, w_qk, nw, cos_tab, sin_tab)


def _inproj_plain_kernel(h_ref, w_ref, o_ref):
    o_ref[...] = jnp.dot(h_ref[...], w_ref[0].astype(BF16),
                         preferred_element_type=F32).astype(o_ref.dtype)


def _inproj_plain(h, w_in, l, col0, ncols, out_dtype, name):
    tn = 512
    off = col0 // tn
    return pl.pallas_call(
        _inproj_plain_kernel,
        out_shape=jax.ShapeDtypeStruct((S_ALL, ncols), out_dtype),
        grid=(S_ALL // TM_W, ncols // tn),
        in_specs=[
            pl.BlockSpec((TM_W, D), lambda i, j: (i, 0)),
            pl.BlockSpec((1, D, tn), lambda i, j: (l, 0, off + j)),
        ],
        out_specs=pl.BlockSpec((TM_W, tn), lambda i, j: (i, j)),
        compiler_params=_params("arbitrary", "arbitrary"),
        name=name,
    )(h, w_in)


def _attn_kernel(bound_ref, q_ref, k_ref, v_ref, o_ref, q4_ref, m_ref, l_ref, acc_ref,
                 *, tq, tk, groups, bounded):
    qi = pl.program_id(1)
    for hh in range(Q_PER_KV):
        q4_ref[hh * tq:(hh + 1) * tq, :] = q_ref[:, hh * HEAD_DIM:(hh + 1) * HEAD_DIM]
    if not bounded:
        m_ref[...] = jnp.full(m_ref.shape, -jnp.inf, F32)
    l_ref[...] = jnp.zeros(l_ref.shape, F32)
    acc_ref[...] = jnp.zeros(acc_ref.shape, F32)
    rows = Q_PER_KV * tq // groups

    def step(start, size):
        k = k_ref[pl.ds(start, size), :]
        v = v_ref[pl.ds(start, size), :]
        rep = size // HEAD_DIM
        scores = [lax.dot_general(q4_ref[r * rows:(r + 1) * rows, :], k, (((1,), (1,)), ((), ())),
                                  preferred_element_type=F32) for r in range(groups)]
        for r in range(groups):
            rs = slice(r * rows, (r + 1) * rows)
            s = scores[r]
            if bounded:
                p = jnp.exp2(s - bound_ref[0])
                lp = p[:, 0:HEAD_DIM]
                for t in range(1, rep):
                    lp = lp + p[:, t * HEAD_DIM:(t + 1) * HEAD_DIM]
                l_ref[rs, :] += lp
                acc_ref[rs, :] += jnp.dot(p.astype(BF16), v, preferred_element_type=F32)
            else:
                m_prev = m_ref[rs, :]
                m_new = jnp.maximum(m_prev, jnp.max(s, axis=-1, keepdims=True))
                alpha = jnp.exp2(m_prev - m_new)
                p = jnp.exp2(s - _lane_tile(m_new, rep))
                l_ref[rs, :] = alpha * l_ref[rs, :] + jnp.sum(p, axis=-1, keepdims=True)
                acc_ref[rs, :] = alpha * acc_ref[rs, :] + jnp.dot(p.astype(BF16), v,
                                                                  preferred_element_type=F32)
                m_ref[rs, :] = m_new

    @pl.when(qi < SEQ // tq)
    def _():
        def body(c, carry):
            step(pl.multiple_of(c * tk, tk), tk)
            return carry
        lax.fori_loop(0, SEQ // tk, body, 0)

    step(SEQ, CTX)
    l = jnp.sum(l_ref[...], axis=-1, keepdims=True) if bounded else l_ref[...]
    o = acc_ref[...] / l
    for hh in range(Q_PER_KV):
        o_ref[:, hh * HEAD_DIM:(hh + 1) * HEAD_DIM] = o[hh * tq:(hh + 1) * tq, :].astype(o_ref.dtype)


def _attention(qk, v, bound, bounded):
    tq, tk, groups = 256, 4096, 4
    gw = Q_PER_KV * HEAD_DIM
    k_blk = D // HEAD_DIM
    return pl.pallas_call(
        functools.partial(_attn_kernel, tq=tq, tk=tk, groups=groups, bounded=bounded),
        out_shape=jax.ShapeDtypeStruct((S_ALL, D), BF16),
        grid=(N_KV, S_ALL // tq),
        in_specs=[
            pl.BlockSpec(memory_space=pltpu.SMEM),
            pl.BlockSpec((tq, gw), lambda g, qi: (qi, g)),
            pl.BlockSpec((S_ALL, HEAD_DIM), lambda g, qi: (0, k_blk + g)),
            pl.BlockSpec((S_ALL, HEAD_DIM), lambda g, qi: (0, g)),
        ],
        out_specs=pl.BlockSpec((tq, gw), lambda g, qi: (qi, g)),
        scratch_shapes=[
            pltpu.VMEM((Q_PER_KV * tq, HEAD_DIM), BF16),
            pltpu.VMEM((Q_PER_KV * tq, HEAD_DIM), F32),
            pltpu.VMEM((Q_PER_KV * tq, HEAD_DIM), F32),
            pltpu.VMEM((Q_PER_KV * tq, HEAD_DIM), F32),
        ],
        compiler_params=_params("arbitrary", "arbitrary"),
        name="attention_bounded" if bounded else "attention_online",
    )(bound, qk, qk, v)


MAX_SCORE_BOUND = 60.0


def _attention_dispatch(qk, v, qn, kn):
    bound = (1.02 * HEAD_DIM * QK_PRESCALE) * jnp.max(jnp.abs(qn)) * jnp.max(jnp.abs(kn))
    b1 = bound.reshape(1).astype(F32)
    return lax.cond(bound <= MAX_SCORE_BOUND,
                    lambda: _attention(qk, v, b1, True),
                    lambda: _attention(qk, v, b1, False))


FFT_R = 8
FFT_M = SEQ // FFT_R


def _four_lat_a_kernel(u_ref, w_ref, zr_ref, zi_ref, r_ref, *, tm):
    r = jnp.dot(u_ref[...].astype(BF16), w_ref[...], preferred_element_type=F32)
    nt = F_GROUP_DIM // 128
    for t in range(2 * nt):
        r_ref[t] = r[:, t * 128:(t + 1) * 128]
    for n1 in range(FFT_R):
        for t in range(2 * nt):
            blk = r_ref[t, pl.ds(n1, tm // FFT_R, stride=FFT_R), :]
            dst = zr_ref if t < nt else zi_ref
            tt = t % nt
            dst[n1, :, tt * 128:(tt + 1) * 128] = blk.astype(BF16)


def _fourier_lat_a(rest, w_c):
    tm = 1024
    off = OFF_FX // F_GROUP_DIM
    out = jax.ShapeDtypeStruct((FFT_R, FFT_M, F_DIM), BF16)
    zspec = pl.BlockSpec((FFT_R, tm // FFT_R, F_GROUP_DIM), lambda i, g: (0, i, g))
    return pl.pallas_call(
        functools.partial(_four_lat_a_kernel, tm=tm),
        out_shape=(out, out),
        grid=(SEQ // tm, F_GROUPS),
        in_specs=[
            pl.BlockSpec((tm, F_GROUP_DIM), lambda i, g: (i, off + g)),
            pl.BlockSpec((F_GROUP_DIM, 2 * F_GROUP_DIM), lambda i, g: (0, 0)),
        ],
        out_specs=(zspec, zspec),
        scratch_shapes=[pltpu.VMEM((2 * F_GROUP_DIM // 128, tm, 128), F32)],
        compiler_params=_params("arbitrary", "arbitrary"),
        name="fourier_lat_a",
    )(rest, w_c)


def _four_lat_b_kernel(w8_ref, c_ref, s_ref, zr_ref, zi_ref, twc_ref, tws_ref, o_ref, acc_ref, *, tc):
    n1 = pl.program_id(1)

    @pl.when(n1 == 0)
    def _():
        acc_ref[...] = jnp.zeros(acc_ref.shape, F32)

    zr = zr_ref[0]
    zi = zi_ref[0]
    cm = c_ref[...]
    sm = s_ref[...]
    vr = (jnp.dot(cm, zr, preferred_element_type=F32) + jnp.dot(sm, zi, preferred_element_type=F32))
    vi = (jnp.dot(cm, zi, preferred_element_type=F32) - jnp.dot(sm, zr, preferred_element_type=F32))
    twc = _lane_tile(twc_ref[0], tc // 128)
    tws = _lane_tile(tws_ref[0], tc // 128)
    pr = vr * twc + vi * tws
    pi = vi * twc - vr * tws
    for k1 in range(FFT_R):
        acc_ref[k1] += w8_ref[0, n1, k1] * pr + w8_ref[1, n1, k1] * pi

    @pl.when(n1 == FFT_R - 1)
    def _():
        scale = 1.0 / float(SEQ * F_GROUP_DIM) ** 0.5
        for k1 in range(FFT_R):
            o_ref[k1 * FFT_M:(k1 + 1) * FFT_M, :] = (acc_ref[k1] * scale).astype(BF16)


def _fourier_lat_b(w8, c_m, s_m, zr, zi, twc, tws):
    tc = 256
    zspec = pl.BlockSpec((1, FFT_M, tc), lambda j, n1: (n1, 0, j))
    twspec = pl.BlockSpec((1, FFT_M, 128), lambda j, n1: (n1, 0, 0))
    mspec = pl.BlockSpec((FFT_M, FFT_M), lambda j, n1: (0, 0))
    return pl.pallas_call(
        functools.partial(_four_lat_b_kernel, tc=tc),
        out_shape=jax.ShapeDtypeStruct((SEQ, F_DIM), BF16),
        grid=(F_DIM // tc, FFT_R),
        in_specs=[pl.BlockSpec(memory_space=pltpu.SMEM), mspec, mspec, zspec, zspec, twspec, twspec],
        out_specs=pl.BlockSpec((SEQ, tc), lambda j, n1: (0, j)),
        scratch_shapes=[pltpu.VMEM((FFT_R, FFT_M, tc), F32)],
        compiler_params=_params("arbitrary", "arbitrary"),
        name="fourier_lat_b",
    )(w8, c_m, s_m, zr, zi, twc, tws)


def _four_ctx_kernel(u_ref, w_ref, c_ref, s_ref, o_ref):
    z = jnp.dot(u_ref[...].astype(BF16), w_ref[...], preferred_element_type=F32)
    zr = z[:, :F_GROUP_DIM].astype(BF16)
    zi = z[:, F_GROUP_DIM:].astype(BF16)
    y = (jnp.dot(c_ref[...], zr, preferred_element_type=F32)
         + jnp.dot(s_ref[...], zi, preferred_element_type=F32))
    o_ref[...] = (y * (1.0 / float(CTX * F_GROUP_DIM) ** 0.5)).astype(BF16)


def _fourier_ctx(rest, w_c, c_x, s_x):
    off = OFF_FX // F_GROUP_DIM
    mspec = pl.BlockSpec((CTX, CTX), lambda g: (0, 0))
    return pl.pallas_call(
        _four_ctx_kernel,
        out_shape=jax.ShapeDtypeStruct((CTX, F_DIM), BF16),
        grid=(F_GROUPS,),
        in_specs=[
            pl.BlockSpec((CTX, F_GROUP_DIM), lambda g: (SEQ // CTX, off + g)),
            pl.BlockSpec((F_GROUP_DIM, 2 * F_GROUP_DIM), lambda g: (0, 0)),
            mspec, mspec,
        ],
        out_specs=pl.BlockSpec((CTX, F_GROUP_DIM), lambda g: (0, g)),
        compiler_params=_params("arbitrary"),
        name="fourier_ctx",
    )(rest, w_c, c_x, s_x)


def _proj_a_kernel(attn_ref, ag_ref, w_ref, o_ref, a_ref):
    @pl.when(pl.program_id(1) == 0)
    def _():
        cw = 512
        for c in range(D // cw):
            sl = slice(c * cw, (c + 1) * cw)
            a_ref[:, sl] = (attn_ref[:, sl].astype(F32) * _silu(ag_ref[:, sl].astype(F32))).astype(BF16)

    o_ref[...] = jnp.dot(a_ref[...], w_ref[0], preferred_element_type=F32)


def _proj_a(attn, rest, w_attn_o, l):
    tm, tn = TM, 1024
    return pl.pallas_call(
        _proj_a_kernel,
        out_shape=jax.ShapeDtypeStruct((S_ALL, D), F32),
        grid=(S_ALL // tm, D // tn),
        in_specs=[
            pl.BlockSpec((tm, D), lambda i, j: (i, 0)),
            pl.BlockSpec((tm, D), lambda i, j: (i, OFF_AG // D)),
            pl.BlockSpec((1, D, tn), lambda i, j: (l, 0, j)),
        ],
        out_specs=pl.BlockSpec((tm, tn), lambda i, j: (i, j)),
        scratch_shapes=[pltpu.VMEM((tm, D), BF16)],
        compiler_params=_params("arbitrary", "arbitrary"),
        name="proj_attn",
    )(attn, rest, w_attn_o)


def _proj_f_kernel(y_ref, fg_ref, wmix_ref, wo_ref, o_ref):
    t = jnp.dot(y_ref[...], wmix_ref[0], preferred_element_type=F32) * _silu(fg_ref[...].astype(F32))
    o_ref[...] = jnp.dot(t.astype(BF16), wo_ref[0], preferred_element_type=F32)


def _proj_f(fmix, rest, w_f_mix, w_f_o, l):
    tm = TM_S
    return pl.pallas_call(
        _proj_f_kernel,
        out_shape=jax.ShapeDtypeStruct((S_ALL, D), F32),
        grid=(S_ALL // tm,),
        in_specs=[
            pl.BlockSpec((tm, F_DIM), lambda i: (i, 0)),
            pl.BlockSpec((tm, F_DIM), lambda i: (i, OFF_FG // F_DIM)),
            pl.BlockSpec((1, F_DIM, F_DIM), lambda i: (l, 0, 0)),
            pl.BlockSpec((1, F_DIM, D), lambda i: (l, 0, 0)),
        ],
        out_specs=pl.BlockSpec((tm, D), lambda i: (i, 0)),
        compiler_params=_params("arbitrary"),
        name="proj_fourier",
    )(fmix, rest, w_f_mix, w_f_o)


def _proj_c_kernel(cx_ref, cb_ref, cc_ref, cg_ref, cxp_ref, ccp_ref, cxn_ref, ccn_ref,
                   cw_ref, cbias_ref, w_ref, o_ref, *, tm):
    hb = HALO_ROWS
    u = cc_ref[...].astype(F32) * cx_ref[...].astype(F32)
    u_before = ccp_ref[hb - 1:hb, :].astype(F32) * cxp_ref[hb - 1:hb, :].astype(F32)
    u_after = ccn_ref[0:1, :].astype(F32) * cxn_ref[0:1, :].astype(F32)
    rows = _row_ids(pl.program_id(0), tm)
    loc = lax.broadcasted_iota(jnp.int32, (tm, 1), 0)
    u_prev = jnp.where(loc == 0, u_before, pltpu.roll(u, 1, 0))
    u_prev = jnp.where((rows == 0) | (rows == SEQ), 0.0, u_prev)
    u_next = jnp.where(loc == tm - 1, u_after, pltpu.roll(u, tm - 1, 0))
    u_next = jnp.where((rows == SEQ - 1) | (rows == S_ALL - 1), 0.0, u_next)
    cw = cw_ref[0]
    conv = u_prev * cw[0:1, :] + u * cw[1:2, :] + u_next * cw[2:3, :] + cbias_ref[0]
    t = cb_ref[...].astype(F32) * conv * _silu(cg_ref[...].astype(F32))
    o_ref[...] = jnp.dot(t.astype(BF16), w_ref[0], preferred_element_type=F32)


HALO_ROWS = 16


def _proj_c(rest, conv_w, conv_b, w_conv_o, l):
    tm = TM_S
    rb = tm // HALO_ROWS
    last = S_ALL // HALO_ROWS - 1

    def col(off):
        return pl.BlockSpec((tm, CONV_DIM), lambda i: (i, off // CONV_DIM))

    def above(off):
        return pl.BlockSpec((HALO_ROWS, CONV_DIM),
                            lambda i: (jnp.maximum(i * rb - 1, 0), off // CONV_DIM))

    def below(off):
        return pl.BlockSpec((HALO_ROWS, CONV_DIM),
                            lambda i: (jnp.minimum((i + 1) * rb, last), off // CONV_DIM))

    return pl.pallas_call(
        functools.partial(_proj_c_kernel, tm=tm),
        out_shape=jax.ShapeDtypeStruct((S_ALL, D), F32),
        grid=(S_ALL // tm,),
        in_specs=[
            col(OFF_CX), col(OFF_CB), col(OFF_CC), col(OFF_CG),
            above(OFF_CX), above(OFF_CC), below(OFF_CX), below(OFF_CC),
            pl.BlockSpec((1, 3, CONV_DIM), lambda i: (l, 0, 0)),
            pl.BlockSpec((1, 1, CONV_DIM), lambda i: (l, 0, 0)),
            pl.BlockSpec((1, CONV_DIM, D), lambda i: (l, 0, 0)),
        ],
        out_specs=pl.BlockSpec((tm, D), lambda i: (i, 0)),
        compiler_params=_params("arbitrary"),
        name="proj_conv",
    )(rest, rest, rest, rest, rest, rest, rest, rest, conv_w, conv_b, w_conv_o)


def _final_kernel(ml0_ref, ml1_ref, ml2_ref, ya_ref, yf_ref, yc_ref, w_ref, x_ref,
                  mod_ref, gp_ref, o_ref, acc_ref, *, tm, nj):
    j = pl.program_id(1)
    m = (_sigmoid(ml0_ref[...].astype(F32)) * ya_ref[...] + _sigmoid(ml1_ref[...].astype(F32)) * yf_ref[...]
         + _sigmoid(ml2_ref[...].astype(F32)) * yc_ref[...])
    part = jnp.dot(m.astype(BF16), w_ref[0], preferred_element_type=F32)

    @pl.when(j == 0)
    def _():
        acc_ref[...] = part

    @pl.when(j > 0)
    def _():
        acc_ref[...] += part

    @pl.when(j == nj - 1)
    def _():
        out = acc_ref[...]
        ms = jnp.mean(out * out, axis=-1, keepdims=True)
        r = out * lax.rsqrt(ms + EPS) * gp_ref[0]
        md = mod_ref[0]
        is_lat = _row_ids(pl.program_id(0), tm) < SEQ
        gate = jnp.where(is_lat, md[0:1, 2 * D:3 * D], md[1:2, 2 * D:3 * D])
        o_ref[...] = x_ref[...] + gate * r


def _final(rest, ya, yf, yc, w_out, x_all, mod, g_post, l):
    tm, tn = TM_S, 512
    nj = D // tn

    def ml(b):
        base = (OFF_ML + b * D) // tn
        return pl.BlockSpec((tm, tn), lambda i, j: (i, base + j))

    ysp = pl.BlockSpec((tm, tn), lambda i, j: (i, j))
    return pl.pallas_call(
        functools.partial(_final_kernel, tm=tm, nj=nj),
        out_shape=jax.ShapeDtypeStruct((S_ALL, D), F32),
        grid=(S_ALL // tm, nj),
        in_specs=[
            ml(0), ml(1), ml(2), ysp, ysp, ysp,
            pl.BlockSpec((1, tn, D), lambda i, j: (l, j, 0)),
            pl.BlockSpec((tm, D), lambda i, j: (i, 0)),
            pl.BlockSpec((1, 8, 3 * D), lambda i, j: (l, 0, 0)),
            pl.BlockSpec((1, 1, D), lambda i, j: (l, 0, 0)),
        ],
        out_specs=pl.BlockSpec((tm, D), lambda i, j: (i, 0)),
        scratch_shapes=[pltpu.VMEM((tm, D), F32)],
        compiler_params=_params("arbitrary", "arbitrary"),
        name="merge_out",
    )(rest, rest, rest, ya, yf, yc, w_out, x_all, mod, g_post)


def _rope_tables():
    half = HEAD_DIM // 4
    pos = jnp.arange(SEQ, dtype=jnp.int32)
    freqs = ROPE_THETA ** (-jnp.arange(half, dtype=F32) / half)
    ang_r = (pos // GRID_W).astype(F32)[:, None] * freqs[None, :]
    ang_c = (pos % GRID_W).astype(F32)[:, None] * freqs[None, :]
    cos = jnp.concatenate([jnp.cos(ang_r), jnp.cos(ang_r), jnp.cos(ang_c), jnp.cos(ang_c)], axis=-1)
    sin = jnp.concatenate([-jnp.sin(ang_r), jnp.sin(ang_r), -jnp.sin(ang_c), jnp.sin(ang_c)], axis=-1)
    cos = jnp.concatenate([cos, jnp.ones((CTX, HEAD_DIM), F32)], axis=0)
    sin = jnp.concatenate([sin, jnp.zeros((CTX, HEAD_DIM), F32)], axis=0)
    return cos, sin


def _dft_cos_sin(n):
    idx = jnp.arange(n, dtype=jnp.int32)
    ang = ((idx[:, None] * idx[None, :]) % n).astype(F32) * (2.0 * jnp.pi / n)
    return jnp.cos(ang), jnp.sin(ang)


def _dft_tables():
    c_c, s_c = _dft_cos_sin(F_GROUP_DIM)
    w_c = jnp.concatenate([c_c, -s_c], axis=1).astype(BF16)
    c_m, s_m = _dft_cos_sin(FFT_M)
    c_x, s_x = _dft_cos_sin(CTX)
    n1 = jnp.arange(FFT_R, dtype=jnp.int32)
    ang = ((n1[:, None] * jnp.arange(FFT_M, dtype=jnp.int32)[None, :]) % SEQ).astype(F32) * (2.0 * jnp.pi / SEQ)
    twc = jnp.broadcast_to(jnp.cos(ang)[:, :, None], (FFT_R, FFT_M, 128))
    tws = jnp.broadcast_to(jnp.sin(ang)[:, :, None], (FFT_R, FFT_M, 128))
    c_8, s_8 = _dft_cos_sin(FFT_R)
    w8 = jnp.stack([c_8, s_8])
    return dict(w_c=w_c, c_m=c_m.astype(BF16), s_m=s_m.astype(BF16), c_x=c_x.astype(BF16),
                s_x=s_x.astype(BF16), twc=twc, tws=tws, w8=w8)


def kernel(x, c, ctx, c_ctx, w_mod, b_mod, g_pre, g_post, w_in, q_norm, k_norm,
           w_attn_o, w_f_mix, w_f_o, conv_w, conv_b, w_conv_o, w_out):
    assert x.shape == (1, SEQ, D) and ctx.shape == (1, CTX, D) and w_in.shape == (DEPTH, D, IN_COLS)
    x_all = jnp.concatenate([x[0], ctx[0]], axis=0)
    c8 = jnp.concatenate([c, c_ctx[None, :], jnp.zeros((6, D), F32)], axis=0)
    cos_tab, sin_tab = _rope_tables()
    ft = _dft_tables()

    w_qk_b = w_in[:, :, :D + KV_DIM].astype(BF16)
    w_attn_o_b = w_attn_o.astype(BF16)
    w_f_mix_b = w_f_mix.astype(BF16)
    w_f_o_b = w_f_o.astype(BF16)
    w_conv_o_b = w_conv_o.astype(BF16)
    w_out_b = w_out.astype(BF16)

    mod = _modulation(c8, w_mod, b_mod.reshape(DEPTH, 1, 3 * D))
    g_pre3 = g_pre.reshape(DEPTH, 1, D)
    g_post3 = g_post.reshape(DEPTH, 1, D)
    conv_b3 = conv_b.reshape(DEPTH, 1, CONV_DIM)

    for l in range(DEPTH):
        nw = jnp.concatenate([jnp.tile(q_norm[l] * QK_PRESCALE, N_HEADS),
                              jnp.tile(k_norm[l], N_KV)])[None, :]
        h = _prenorm(x_all, mod, g_pre3, l)
        qk = _inproj_qk(h, w_qk_b, nw, cos_tab, sin_tab, l)
        v = _inproj_plain(h, w_in, l, D + KV_DIM, KV_DIM, BF16, "inproj_v")
        rest = _inproj_plain(h, w_in, l, QKV_COLS, REST_COLS, BF16, "inproj_rest")
        attn = _attention_dispatch(qk, v, q_norm[l], k_norm[l])
        zr, zi = _fourier_lat_a(rest, ft["w_c"])
        fmix_lat = _fourier_lat_b(ft["w8"], ft["c_m"], ft["s_m"], zr, zi, ft["twc"], ft["tws"])
        fmix_ctx = _fourier_ctx(rest, ft["w_c"], ft["c_x"], ft["s_x"])
        fmix = jnp.concatenate([fmix_lat, fmix_ctx], axis=0)
        ya = _proj_a(attn, rest, w_attn_o_b, l)
        yf = _proj_f(fmix, rest, w_f_mix_b, w_f_o_b, l)
        yc = _proj_c(rest, conv_w, conv_b3, w_conv_o_b, l)
        x_all = _final(rest, ya, yf, yc, w_out_b, x_all, mod, g_post3, l)

    return x_all[:SEQ][None]
```

```python
import functools

import jax
import jax.numpy as jnp
from jax import lax
from jax.experimental import pallas as pl
from jax.experimental.pallas import tpu as pltpu

D = 2048
SEQ = 8192
CTX = 256
S_ALL = SEQ + CTX
DEPTH = 4
GRID_W = 64
HEAD_DIM = 128
N_HEADS = 16
N_KV = 4
Q_PER_KV = N_HEADS // N_KV
KV_DIM = N_KV * HEAD_DIM
F_GROUPS = 4
F_DIM = 1024
F_GROUP_DIM = F_DIM // F_GROUPS
CONV_DIM = 1024
ROPE_THETA = 10000.0
EPS = 1e-6
QK_PRESCALE = 1.4426950408889634 / float(HEAD_DIM) ** 0.5
IN_COLS = 17408
QKV_COLS = D + 2 * KV_DIM
REST_COLS = IN_COLS - QKV_COLS
OFF_AG, OFF_FX, OFF_FG, OFF_CX, OFF_CB, OFF_CC, OFF_CG, OFF_ML = (
    0, 2048, 3072, 4096, 5120, 6144, 7168, 8192)

TM = 1056
TM_W = 2112
TM_S = 528
VMEM_LIMIT_BYTES = 56 * 1024 * 1024

F32 = jnp.float32
BF16 = jnp.bfloat16


def _params(*sem):
    return pltpu.CompilerParams(dimension_semantics=sem, vmem_limit_bytes=VMEM_LIMIT_BYTES)


def _sigmoid(v):
    return 0.5 * jnp.tanh(0.5 * v) + 0.5


def _silu(v):
    return v * _sigmoid(v)


def _lane_tile(v, rep):
    return v if rep == 1 else jnp.concatenate([v] * rep, axis=1)


def _row_ids(i, tm):
    return i * tm + lax.broadcasted_iota(jnp.int32, (tm, 1), 0)


def _mod_kernel(c_ref, w_ref, b_ref, o_ref):
    sc = _silu(c_ref[...])
    o_ref[0] = jnp.dot(sc.astype(BF16), w_ref[0].astype(BF16), preferred_element_type=F32) + b_ref[0]


def _modulation(c8, w_mod, b_mod):
    tn = 1024
    return pl.pallas_call(
        _mod_kernel,
        out_shape=jax.ShapeDtypeStruct((DEPTH, 8, 3 * D), F32),
        grid=(DEPTH, 3 * D // tn),
        in_specs=[
            pl.BlockSpec((8, D), lambda l, j: (0, 0)),
            pl.BlockSpec((1, D, tn), lambda l, j: (l, 0, j)),
            pl.BlockSpec((1, 1, tn), lambda l, j: (l, 0, j)),
        ],
        out_specs=pl.BlockSpec((1, 8, tn), lambda l, j: (l, 0, j)),
        compiler_params=_params("arbitrary", "arbitrary"),
        name="modulation",
    )(c8, w_mod, b_mod)


def _prenorm_kernel(x_ref, mod_ref, g_ref, h_ref, *, tm):
    m = mod_ref[0]
    g = g_ref[0]
    base = pl.program_id(0) * tm
    rc = 16

    def body(r, carry):
        r0 = pl.multiple_of(r * rc, rc)
        x = x_ref[pl.ds(r0, rc), :]
        ms = jnp.mean(x * x, axis=-1, keepdims=True)
        y = x * lax.rsqrt(ms + EPS) * g
        is_lat = (base + r0) < SEQ
        shift = jnp.where(is_lat, m[0:1, 0:D], m[1:2, 0:D])
        scale = jnp.where(is_lat, m[0:1, D:2 * D], m[1:2, D:2 * D])
        h_ref[pl.ds(r0, rc), :] = (y * (1.0 + scale) + shift).astype(BF16)
        return carry

    lax.fori_loop(0, tm // rc, body, 0, unroll=11)


def _prenorm(x_all, mod, g_pre, l):
    tm = TM_S
    return pl.pallas_call(
        functools.partial(_prenorm_kernel, tm=tm),
        out_shape=jax.ShapeDtypeStruct((S_ALL, D), BF16),
        grid=(S_ALL // tm,),
        in_specs=[
            pl.BlockSpec((tm, D), lambda i: (i, 0)),
            pl.BlockSpec((1, 8, 3 * D), lambda i: (l, 0, 0)),
            pl.BlockSpec((1, 1, D), lambda i: (l, 0, 0)),
        ],
        out_specs=pl.BlockSpec((tm, D), lambda i: (i, 0)),
        compiler_params=_params("arbitrary"),
        name="prenorm",
    )(x_all, mod, g_pre)


def _swap_halves(t):
    lane = lax.broadcasted_iota(jnp.int32, t.shape, 1)
    return jnp.where((lane % 64) < 32, pltpu.roll(t, 96, 1), pltpu.roll(t, 32, 1))


def _inproj_qk_kernel(h_ref, w_ref, nw_ref, cos_ref, sin_ref, o_ref, acc_a, acc_b, *, tn):
    i = pl.program_id(0)
    j = pl.program_id(1)

    @pl.when((i == 0) & (j == 0))
    def _():
        acc_b[...] = jnp.zeros(acc_b.shape, F32)

    def stage(acc_new, acc_old):
        acc_new[...] = jnp.dot(h_ref[...], w_ref[0], preferred_element_type=F32)
        cos = cos_ref[...]
        sin = sin_ref[...]
        for hh in range(tn // HEAD_DIM):
            sl = slice(hh * HEAD_DIM, (hh + 1) * HEAD_DIM)
            t = acc_old[:, sl]
            ms = jnp.mean(t * t, axis=-1, keepdims=True)
            y = t * lax.rsqrt(ms + EPS) * nw_ref[:, sl]
            o_ref[:, sl] = (y * cos + _swap_halves(y) * sin).astype(BF16)

    @pl.when(j % 2 == 0)
    def _():
        stage(acc_a, acc_b)

    @pl.when(j % 2 == 1)
    def _():
        stage(acc_b, acc_a)


def _inproj_qk(h, w_qk, nw, cos_tab, sin_tab, l):
    tn = 512
    nt = (D + KV_DIM) // tn
    return pl.pallas_call(
        functools.partial(_inproj_qk_kernel, tn=tn),
        out_shape=jax.ShapeDtypeStruct((S_ALL, D + KV_DIM), BF16),
        grid=(S_ALL // TM, nt + 1),
        in_specs=[
            pl.BlockSpec((TM, D), lambda i, j: (i, 0)),
            pl.BlockSpec((1, D, tn), lambda i, j: (l, 0, jnp.minimum(j, nt - 1))),
            pl.BlockSpec((1, tn), lambda i, j: (0, jnp.maximum(j - 1, 0))),
            pl.BlockSpec((TM, HEAD_DIM), lambda i, j: (i, 0)),
            pl.BlockSpec((TM, HEAD_DIM), lambda i, j: (i, 0)),
        ],
        out_specs=pl.BlockSpec((TM, tn), lambda i, j: (i, jnp.maximum(j - 1, 0))),
        scratch_shapes=[pltpu.VMEM((TM, tn), F32), pltpu.VMEM((TM, tn), F32)],
        compiler_params=_params("arbitrary", "arbitrary"),
        name="inproj_qk",
    )(h, w_qk, nw, cos_tab, sin_tab)


def _inproj_plain_kernel(h_ref, w_ref, o_ref):
    o_ref[...] = jnp.dot(h_ref[...], w_ref[0].astype(BF16),
                         preferred_element_type=F32).astype(o_ref.dtype)


def _inproj_plain(h, w_in, l, col0, ncols, out_dtype, name):
    tn = 512
    off = col0 // tn
    return pl.pallas_call(
        _inproj_plain_kernel,
        out_shape=jax.ShapeDtypeStruct((S_ALL, ncols), out_dtype),
        grid=(S_ALL // TM_W, ncols // tn),
        in_specs=[
            pl.BlockSpec((TM_W, D), lambda i, j: (i, 0)),
            pl.BlockSpec((1, D, tn), lambda i, j: (l, 0, off + j)),
        ],
        out_specs=pl.BlockSpec((TM_W, tn), lambda i, j: (i, j)),
        compiler_params=_params("arbitrary", "arbitrary"),
        name=name,
    )(h, w_in)


def _attn_kernel(bound_ref, q_ref, k_ref, v_ref, o_ref, q4_ref, m_ref, l_ref, acc_ref,
                 *, tq, tk, groups, bounded):
    qi = pl.program_id(1)
    for hh in range(Q_PER_KV):
        q4_ref[hh * tq:(hh + 1) * tq, :] = q_ref[:, hh * HEAD_DIM:(hh + 1) * HEAD_DIM]
    if not bounded:
        m_ref[...] = jnp.full(m_ref.shape, -jnp.inf, F32)
    l_ref[...] = jnp.zeros(l_ref.shape, F32)
    acc_ref[...] = jnp.zeros(acc_ref.shape, F32)
    rows = Q_PER_KV * tq // groups

    def step(start, size):
        k = k_ref[pl.ds(start, size), :]
        v = v_ref[pl.ds(start, size), :]
        rep = size // HEAD_DIM
        scores = [lax.dot_general(q4_ref[r * rows:(r + 1) * rows, :], k, (((1,), (1,)), ((), ())),
                                  preferred_element_type=F32) for r in range(groups)]
        for r in range(groups):
            rs = slice(r * rows, (r + 1) * rows)
            s = scores[r]
            if bounded:
                p = jnp.exp2(s - bound_ref[0])
                lp = p[:, 0:HEAD_DIM]
                for t in range(1, rep):
                    lp = lp + p[:, t * HEAD_DIM:(t + 1) * HEAD_DIM]
                l_ref[rs, :] += lp
                acc_ref[rs, :] += jnp.dot(p.astype(BF16), v, preferred_element_type=F32)
            else:
                m_prev = m_ref[rs, :]
                m_new = jnp.maximum(m_prev, jnp.max(s, axis=-1, keepdims=True))
                alpha = jnp.exp2(m_prev - m_new)
                p = jnp.exp2(s - _lane_tile(m_new, rep))
                l_ref[rs, :] = alpha * l_ref[rs, :] + jnp.sum(p, axis=-1, keepdims=True)
                acc_ref[rs, :] = alpha * acc_ref[rs, :] + jnp.dot(p.astype(BF16), v,
                                                                  preferred_element_type=F32)
                m_ref[rs, :] = m_new

    @pl.when(qi < SEQ // tq)
    def _():
        def body(c, carry):
            step(pl.multiple_of(c * tk, tk), tk)
            return carry
        lax.fori_loop(0, SEQ // tk, body, 0)

    step(SEQ, CTX)
    l = jnp.sum(l_ref[...], axis=-1, keepdims=True) if bounded else l_ref[...]
    o = acc_ref[...] / l
    for hh in range(Q_PER_KV):
        o_ref[:, hh * HEAD_DIM:(hh + 1) * HEAD_DIM] = o[hh * tq:(hh + 1) * tq, :].astype(o_ref.dtype)


def _attention(qk, v, bound, bounded):
    tq, tk, groups = 256, 4096, 4
    gw = Q_PER_KV * HEAD_DIM
    k_blk = D // HEAD_DIM
    return pl.pallas_call(
        functools.partial(_attn_kernel, tq=tq, tk=tk, groups=groups, bounded=bounded),
        out_shape=jax.ShapeDtypeStruct((S_ALL, D), BF16),
        grid=(N_KV, S_ALL // tq),
        in_specs=[
            pl.BlockSpec(memory_space=pltpu.SMEM),
            pl.BlockSpec((tq, gw), lambda g, qi: (qi, g)),
            pl.BlockSpec((S_ALL, HEAD_DIM), lambda g, qi: (0, k_blk + g)),
            pl.BlockSpec((S_ALL, HEAD_DIM), lambda g, qi: (0, g)),
        ],
        out_specs=pl.BlockSpec((tq, gw), lambda g, qi: (qi, g)),
        scratch_shapes=[
            pltpu.VMEM((Q_PER_KV * tq, HEAD_DIM), BF16),
            pltpu.VMEM((Q_PER_KV * tq, HEAD_DIM), F32),
            pltpu.VMEM((Q_PER_KV * tq, HEAD_DIM), F32),
            pltpu.VMEM((Q_PER_KV * tq, HEAD_DIM), F32),
        ],
        compiler_params=_params("arbitrary", "arbitrary"),
        name="attention_bounded" if bounded else "attention_online",
    )(bound, qk, qk, v)


MAX_SCORE_BOUND = 60.0


def _attention_dispatch(qk, v, qn, kn):
    bound = (1.02 * HEAD_DIM * QK_PRESCALE) * jnp.max(jnp.abs(qn)) * jnp.max(jnp.abs(kn))
    b1 = bound.reshape(1).astype(F32)
    return lax.cond(bound <= MAX_SCORE_BOUND,
                    lambda: _attention(qk, v, b1, True),
                    lambda: _attention(qk, v, b1, False))


FFT_R = 8
FFT_M = SEQ // FFT_R


def _four_lat_a_kernel(u_ref, w_ref, zr_ref, zi_ref, r_ref, *, tm):
    r = jnp.dot(u_ref[...].astype(BF16), w_ref[...], preferred_element_type=F32)
    nt = F_GROUP_DIM // 128
    for t in range(2 * nt):
        r_ref[t] = r[:, t * 128:(t + 1) * 128]
    for n1 in range(FFT_R):
        for t in range(2 * nt):
            blk = r_ref[t, pl.ds(n1, tm // FFT_R, stride=FFT_R), :]
            dst = zr_ref if t < nt else zi_ref
            tt = t % nt
            dst[n1, :, tt * 128:(tt + 1) * 128] = blk.astype(BF16)


def _fourier_lat_a(rest, w_c):
    tm = 1024
    off = OFF_FX // F_GROUP_DIM
    out = jax.ShapeDtypeStruct((FFT_R, FFT_M, F_DIM), BF16)
    zspec = pl.BlockSpec((FFT_R, tm // FFT_R, F_GROUP_DIM), lambda i, g: (0, i, g))
    return pl.pallas_call(
        functools.partial(_four_lat_a_kernel, tm=tm),
        out_shape=(out, out),
        grid=(SEQ // tm, F_GROUPS),
        in_specs=[
            pl.BlockSpec((tm, F_GROUP_DIM), lambda i, g: (i, off + g)),
            pl.BlockSpec((F_GROUP_DIM, 2 * F_GROUP_DIM), lambda i, g: (0, 0)),
        ],
        out_specs=(zspec, zspec),
        scratch_shapes=[pltpu.VMEM((2 * F_GROUP_DIM // 128, tm, 128), F32)],
        compiler_params=_params("arbitrary", "arbitrary"),
        name="fourier_lat_a",
    )(rest, w_c)


def _four_lat_b_kernel(w8_ref, c_ref, s_ref, zr_ref, zi_ref, twc_ref, tws_ref, o_ref, acc_ref, *, tc):
    n1 = pl.program_id(1)

    @pl.when(n1 == 0)
    def _():
        acc_ref[...] = jnp.zeros(acc_ref.shape, F32)

    zr = zr_ref[0]
    zi = zi_ref[0]
    cm = c_ref[...]
    sm = s_ref[...]
    vr = (jnp.dot(cm, zr, preferred_element_type=F32) + jnp.dot(sm, zi, preferred_element_type=F32))
    vi = (jnp.dot(cm, zi, preferred_element_type=F32) - jnp.dot(sm, zr, preferred_element_type=F32))
    twc = _lane_tile(twc_ref[0], tc // 128)
    tws = _lane_tile(tws_ref[0], tc // 128)
    pr = vr * twc + vi * tws
    pi = vi * twc - vr * tws
    for k1 in range(FFT_R):
        acc_ref[k1] += w8_ref[0, n1, k1] * pr + w8_ref[1, n1, k1] * pi

    @pl.when(n1 == FFT_R - 1)
    def _():
        scale = 1.0 / float(SEQ * F_GROUP_DIM) ** 0.5
        for k1 in range(FFT_R):
            o_ref[k1 * FFT_M:(k1 + 1) * FFT_M, :] = (acc_ref[k1] * scale).astype(BF16)


def _fourier_lat_b(w8, c_m, s_m, zr, zi, twc, tws):
    tc = 256
    zspec = pl.BlockSpec((1, FFT_M, tc), lambda j, n1: (n1, 0, j))
    twspec = pl.BlockSpec((1, FFT_M, 128), lambda j, n1: (n1, 0, 0))
    mspec = pl.BlockSpec((FFT_M, FFT_M), lambda j, n1: (0, 0))
    return pl.pallas_call(
        functools.partial(_four_lat_b_kernel, tc=tc),
        out_shape=jax.ShapeDtypeStruct((SEQ, F_DIM), BF16),
        grid=(F_DIM // tc, FFT_R),
        in_specs=[pl.BlockSpec(memory_space=pltpu.SMEM), mspec, mspec, zspec, zspec, twspec, twspec],
        out_specs=pl.BlockSpec((SEQ, tc), lambda j, n1: (0, j)),
        scratch_shapes=[pltpu.VMEM((FFT_R, FFT_M, tc), F32)],
        compiler_params=_params("arbitrary", "arbitrary"),
        name="fourier_lat_b",
    )(w8, c_m, s_m, zr, zi, twc, tws)


def _four_ctx_kernel(u_ref, w_ref, c_ref, s_ref, o_ref):
    z = jnp.dot(u_ref[...].astype(BF16), w_ref[...], preferred_element_type=F32)
    zr = z[:, :F_GROUP_DIM].astype(BF16)
    zi = z[:, F_GROUP_DIM:].astype(BF16)
    y = (jnp.dot(c_ref[...], zr, preferred_element_type=F32)
         + jnp.dot(s_ref[...], zi, preferred_element_type=F32))
    o_ref[...] = (y * (1.0 / float(CTX * F_GROUP_DIM) ** 0.5)).astype(BF16)


def _fourier_ctx(rest, w_c, c_x, s_x):
    off = OFF_FX // F_GROUP_DIM
    mspec = pl.BlockSpec((CTX, CTX), lambda g: (0, 0))
    return pl.pallas_call(
        _four_ctx_kernel,
        out_shape=jax.ShapeDtypeStruct((CTX, F_DIM), BF16),
        grid=(F_GROUPS,),
        in_specs=[
            pl.BlockSpec((CTX, F_GROUP_DIM), lambda g: (SEQ // CTX, off + g)),
            pl.BlockSpec((F_GROUP_DIM, 2 * F_GROUP_DIM), lambda g: (0, 0)),
            mspec, mspec,
        ],
        out_specs=pl.BlockSpec((CTX, F_GROUP_DIM), lambda g: (0, g)),
        compiler_params=_params("arbitrary"),
        name="fourier_ctx",
    )(rest, w_c, c_x, s_x)


def _proj_a_kernel(attn_ref, ag_ref, w_ref, o_ref, a_ref):
    @pl.when(pl.program_id(1) == 0)
    def _():
        cw = 512
        for c in range(D // cw):
            sl = slice(c * cw, (c + 1) * cw)
            a_ref[:, sl] = (attn_ref[:, sl].astype(F32) * _silu(ag_ref[:, sl].astype(F32))).astype(BF16)

    o_ref[...] = jnp.dot(a_ref[...], w_ref[0], preferred_element_type=F32)


def _proj_a(attn, rest, w_attn_o, l):
    tm, tn = TM, 1024
    return pl.pallas_call(
        _proj_a_kernel,
        out_shape=jax.ShapeDtypeStruct((S_ALL, D), F32),
        grid=(S_ALL // tm, D // tn),
        in_specs=[
            pl.BlockSpec((tm, D), lambda i, j: (i, 0)),
            pl.BlockSpec((tm, D), lambda i, j: (i, OFF_AG // D)),
            pl.BlockSpec((1, D, tn), lambda i, j: (l, 0, j)),
        ],
        out_specs=pl.BlockSpec((tm, tn), lambda i, j: (i, j)),
        scratch_shapes=[pltpu.VMEM((tm, D), BF16)],
        compiler_params=_params("arbitrary", "arbitrary"),
        name="proj_attn",
    )(attn, rest, w_attn_o)


def _proj_f_kernel(y_ref, fg_ref, wmix_ref, wo_ref, o_ref):
    t = jnp.dot(y_ref[...], wmix_ref[0], preferred_element_type=F32) * _silu(fg_ref[...].astype(F32))
    o_ref[...] = jnp.dot(t.astype(BF16), wo_ref[0], preferred_element_type=F32)


def _proj_f(fmix, rest, w_f_mix, w_f_o, l):
    tm = TM_S
    return pl.pallas_call(
        _proj_f_kernel,
        out_shape=jax.ShapeDtypeStruct((S_ALL, D), F32),
        grid=(S_ALL // tm,),
        in_specs=[
            pl.BlockSpec((tm, F_DIM), lambda i: (i, 0)),
            pl.BlockSpec((tm, F_DIM), lambda i: (i, OFF_FG // F_DIM)),
            pl.BlockSpec((1, F_DIM, F_DIM), lambda i: (l, 0, 0)),
            pl.BlockSpec((1, F_DIM, D), lambda i: (l, 0, 0)),
        ],
        out_specs=pl.BlockSpec((tm, D), lambda i: (i, 0)),
        compiler_params=_params("arbitrary"),
        name="proj_fourier",
    )(fmix, rest, w_f_mix, w_f_o)


def _proj_c_kernel(cx_ref, cb_ref, cc_ref, cg_ref, cxp_ref, ccp_ref, cxn_ref, ccn_ref,
                   cw_ref, cbias_ref, w_ref, o_ref, *, tm):
    hb = HALO_ROWS
    u = cc_ref[...].astype(F32) * cx_ref[...].astype(F32)
    u_before = ccp_ref[hb - 1:hb, :].astype(F32) * cxp_ref[hb - 1:hb, :].astype(F32)
    u_after = ccn_ref[0:1, :].astype(F32) * cxn_ref[0:1, :].astype(F32)
    rows = _row_ids(pl.program_id(0), tm)
    loc = lax.broadcasted_iota(jnp.int32, (tm, 1), 0)
    u_prev = jnp.where(loc == 0, u_before, pltpu.roll(u, 1, 0))
    u_prev = jnp.where((rows == 0) | (rows == SEQ), 0.0, u_prev)
    u_next = jnp.where(loc == tm - 1, u_after, pltpu.roll(u, tm - 1, 0))
    u_next = jnp.where((rows == SEQ - 1) | (rows == S_ALL - 1), 0.0, u_next)
    cw = cw_ref[0]
    conv = u_prev * cw[0:1, :] + u * cw[1:2, :] + u_next * cw[2:3, :] + cbias_ref[0]
    t = cb_ref[...].astype(F32) * conv * _silu(cg_ref[...].astype(F32))
    o_ref[...] = jnp.dot(t.astype(BF16), w_ref[0], preferred_element_type=F32)


HALO_ROWS = 16


def _proj_c(rest, conv_w, conv_b, w_conv_o, l):
    tm = TM_S
    rb = tm // HALO_ROWS
    last = S_ALL // HALO_ROWS - 1

    def col(off):
        return pl.BlockSpec((tm, CONV_DIM), lambda i: (i, off // CONV_DIM))

    def above(off):
        return pl.BlockSpec((HALO_ROWS, CONV_DIM),
                            lambda i: (jnp.maximum(i * rb - 1, 0), off // CONV_DIM))

    def below(off):
        return pl.BlockSpec((HALO_ROWS, CONV_DIM),
                            lambda i: (jnp.minimum((i + 1) * rb, last), off // CONV_DIM))

    return pl.pallas_call(
        functools.partial(_proj_c_kernel, tm=tm),
        out_shape=jax.ShapeDtypeStruct((S_ALL, D), F32),
        grid=(S_ALL // tm,),
        in_specs=[
            col(OFF_CX), col(OFF_CB), col(OFF_CC), col(OFF_CG),
            above(OFF_CX), above(OFF_CC), below(OFF_CX), below(OFF_CC),
            pl.BlockSpec((1, 3, CONV_DIM), lambda i: (l, 0, 0)),
            pl.BlockSpec((1, 1, CONV_DIM), lambda i: (l, 0, 0)),
            pl.BlockSpec((1, CONV_DIM, D), lambda i: (l, 0, 0)),
        ],
        out_specs=pl.BlockSpec((tm, D), lambda i: (i, 0)),
        compiler_params=_params("arbitrary"),
        name="proj_conv",
    )(rest, rest, rest, rest, rest, rest, rest, rest, conv_w, conv_b, w_conv_o)


def _final_kernel(ml0_ref, ml1_ref, ml2_ref, ya_ref, yf_ref, yc_ref, w_ref, x_ref,
                  mod_ref, gp_ref, o_ref, acc_ref, *, tm, nj):
    j = pl.program_id(1)

    @pl.when(j == 0)
    def _():
        acc_ref[...] = jnp.zeros(acc_ref.shape, F32)

    m = (_sigmoid(ml0_ref[...].astype(F32)) * ya_ref[...] + _sigmoid(ml1_ref[...].astype(F32)) * yf_ref[...]
         + _sigmoid(ml2_ref[...].astype(F32)) * yc_ref[...])
    acc_ref[...] += jnp.dot(m.astype(BF16), w_ref[0], preferred_element_type=F32)

    @pl.when(j == nj - 1)
    def _():
        out = acc_ref[...]
        ms = jnp.mean(out * out, axis=-1, keepdims=True)
        r = out * lax.rsqrt(ms + EPS) * gp_ref[0]
        md = mod_ref[0]
        is_lat = _row_ids(pl.program_id(0), tm) < SEQ
        gate = jnp.where(is_lat, md[0:1, 2 * D:3 * D], md[1:2, 2 * D:3 * D])
        o_ref[...] = x_ref[...] + gate * r


def _final(rest, ya, yf, yc, w_out, x_all, mod, g_post, l):
    tm, tn = TM_S, 512
    nj = D // tn

    def ml(b):
        base = (OFF_ML + b * D) // tn
        return pl.BlockSpec((tm, tn), lambda i, j: (i, base + j))

    ysp = pl.BlockSpec((tm, tn), lambda i, j: (i, j))
    return pl.pallas_call(
        functools.partial(_final_kernel, tm=tm, nj=nj),
        out_shape=jax.ShapeDtypeStruct((S_ALL, D), F32),
        grid=(S_ALL // tm, nj),
        in_specs=[
            ml(0), ml(1), ml(2), ysp, ysp, ysp,
            pl.BlockSpec((1, tn, D), lambda i, j: (l, j, 0)),
            pl.BlockSpec((tm, D), lambda i, j: (i, 0)),
            pl.BlockSpec((1, 8, 3 * D), lambda i, j: (l, 0, 0)),
            pl.BlockSpec((1, 1, D), lambda i, j: (l, 0, 0)),
        ],
        out_specs=pl.BlockSpec((tm, D), lambda i, j: (i, 0)),
        scratch_shapes=[pltpu.VMEM((tm, D), F32)],
        compiler_params=_params("arbitrary", "arbitrary"),
        name="merge_out",
    )(rest, rest, rest, ya, yf, yc, w_out, x_all, mod, g_post)


def _rope_tables():
    half = HEAD_DIM // 4
    pos = jnp.arange(SEQ, dtype=jnp.int32)
    freqs = ROPE_THETA ** (-jnp.arange(half, dtype=F32) / half)
    ang_r = (pos // GRID_W).astype(F32)[:, None] * freqs[None, :]
    ang_c = (pos % GRID_W).astype(F32)[:, None] * freqs[None, :]
    cos = jnp.concatenate([jnp.cos(ang_r), jnp.cos(ang_r), jnp.cos(ang_c), jnp.cos(ang_c)], axis=-1)
    sin = jnp.concatenate([-jnp.sin(ang_r), jnp.sin(ang_r), -jnp.sin(ang_c), jnp.sin(ang_c)], axis=-1)
    cos = jnp.concatenate([cos, jnp.ones((CTX, HEAD_DIM), F32)], axis=0)
    sin = jnp.concatenate([sin, jnp.zeros((CTX, HEAD_DIM), F32)], axis=0)
    return cos, sin


def _dft_cos_sin(n):
    idx = jnp.arange(n, dtype=jnp.int32)
    ang = ((idx[:, None] * idx[None, :]) % n).astype(F32) * (2.0 * jnp.pi / n)
    return jnp.cos(ang), jnp.sin(ang)


def _dft_tables():
    c_c, s_c = _dft_cos_sin(F_GROUP_DIM)
    w_c = jnp.concatenate([c_c, -s_c], axis=1).astype(BF16)
    c_m, s_m = _dft_cos_sin(FFT_M)
    c_x, s_x = _dft_cos_sin(CTX)
    n1 = jnp.arange(FFT_R, dtype=jnp.int32)
    ang = ((n1[:, None] * jnp.arange(FFT_M, dtype=jnp.int32)[None, :]) % SEQ).astype(F32) * (2.0 * jnp.pi / SEQ)
    twc = jnp.broadcast_to(jnp.cos(ang)[:, :, None], (FFT_R, FFT_M, 128))
    tws = jnp.broadcast_to(jnp.sin(ang)[:, :, None], (FFT_R, FFT_M, 128))
    c_8, s_8 = _dft_cos_sin(FFT_R)
    w8 = jnp.stack([c_8, s_8])
    return dict(w_c=w_c, c_m=c_m.astype(BF16), s_m=s_m.astype(BF16), c_x=c_x.astype(BF16),
                s_x=s_x.astype(BF16), twc=twc, tws=tws, w8=w8)


def kernel(x, c, ctx, c_ctx, w_mod, b_mod, g_pre, g_post, w_in, q_norm, k_norm,
           w_attn_o, w_f_mix, w_f_o, conv_w, conv_b, w_conv_o, w_out):
    assert x.shape == (1, SEQ, D) and ctx.shape == (1, CTX, D) and w_in.shape == (DEPTH, D, IN_COLS)
    x_all = jnp.concatenate([x[0], ctx[0]], axis=0)
    c8 = jnp.concatenate([c, c_ctx[None, :], jnp.zeros((6, D), F32)], axis=0)
    cos_tab, sin_tab = _rope_tables()
    ft = _dft_tables()

    w_qk_b = w_in[:, :, :D + KV_DIM].astype(BF16)
    w_attn_o_b = w_attn_o.astype(BF16)
    w_f_mix_b = w_f_mix.astype(BF16)
    w_f_o_b = w_f_o.astype(BF16)
    w_conv_o_b = w_conv_o.astype(BF16)
    w_out_b = w_out.astype(BF16)

    mod = _modulation(c8, w_mod, b_mod.reshape(DEPTH, 1, 3 * D))
    g_pre3 = g_pre.reshape(DEPTH, 1, D)
    g_post3 = g_post.reshape(DEPTH, 1, D)
    conv_b3 = conv_b.reshape(DEPTH, 1, CONV_DIM)

    for l in range(DEPTH):
        nw = jnp.concatenate([jnp.tile(q_norm[l] * QK_PRESCALE, N_HEADS),
                              jnp.tile(k_norm[l], N_KV)])[None, :]
        h = _prenorm(x_all, mod, g_pre3, l)
        qk = _inproj_qk(h, w_qk_b, nw, cos_tab, sin_tab, l)
        v = _inproj_plain(h, w_in, l, D + KV_DIM, KV_DIM, BF16, "inproj_v")
        rest = _inproj_plain(h, w_in, l, QKV_COLS, REST_COLS, BF16, "inproj_rest")
        attn = _attention_dispatch(qk, v, q_norm[l], k_norm[l])
        zr, zi = _fourier_lat_a(rest, ft["w_c"])
        fmix_lat = _fourier_lat_b(ft["w8"], ft["c_m"], ft["s_m"], zr, zi, ft["twc"], ft["tws"])
        fmix_ctx = _fourier_ctx(rest, ft["w_c"], ft["c_x"], ft["s_x"])
        fmix = jnp.concatenate([fmix_lat, fmix_ctx], axis=0)
        ya = _proj_a(attn, rest, w_attn_o_b, l)
        yf = _proj_f(fmix, rest, w_f_mix_b, w_f_o_b, l)
        yc = _proj_c(rest, conv_w, conv_b3, w_conv_o_b, l)
        x_all = _final(rest, ya, yf, yc, w_out_b, x_all, mod, g_post3, l)

    return x_all[:SEQ][None]
```

```python
import functools

import jax
import jax.numpy as jnp
from jax import lax
from jax.experimental import pallas as pl
from jax.experimental.pallas import tpu as pltpu

D = 2048
SEQ = 8192
CTX = 256
S_ALL = SEQ + CTX
DEPTH = 4
GRID_W = 64
HEAD_DIM = 128
N_HEADS = 16
N_KV = 4
Q_PER_KV = N_HEADS // N_KV
KV_DIM = N_KV * HEAD_DIM
F_GROUPS = 4
F_DIM = 1024
F_GROUP_DIM = F_DIM // F_GROUPS
CONV_DIM = 1024
ROPE_THETA = 10000.0
EPS = 1e-6
QK_PRESCALE = 1.4426950408889634 / float(HEAD_DIM) ** 0.5
IN_COLS = 17408
QKV_COLS = D + 2 * KV_DIM
REST_COLS = IN_COLS - QKV_COLS
OFF_AG, OFF_FX, OFF_FG, OFF_CX, OFF_CB, OFF_CC, OFF_CG, OFF_ML = (
    0, 2048, 3072, 4096, 5120, 6144, 7168, 8192)

TM = 1056
TM_W = 2112
TM_S = 528
VMEM_LIMIT_BYTES = 56 * 1024 * 1024

F32 = jnp.float32
BF16 = jnp.bfloat16


def _params(*sem):
    return pltpu.CompilerParams(dimension_semantics=sem, vmem_limit_bytes=VMEM_LIMIT_BYTES)


def _sigmoid(v):
    return 0.5 * jnp.tanh(0.5 * v) + 0.5


def _silu(v):
    return v * _sigmoid(v)


def _lane_tile(v, rep):
    return v if rep == 1 else jnp.concatenate([v] * rep, axis=1)


def _row_ids(i, tm):
    return i * tm + lax.broadcasted_iota(jnp.int32, (tm, 1), 0)


def _mod_kernel(c_ref, w_ref, b_ref, o_ref):
    sc = _silu(c_ref[...])
    o_ref[0] = jnp.dot(sc.astype(BF16), w_ref[0].astype(BF16), preferred_element_type=F32) + b_ref[0]


def _modulation(c8, w_mod, b_mod):
    tn = 1024
    return pl.pallas_call(
        _mod_kernel,
        out_shape=jax.ShapeDtypeStruct((DEPTH, 8, 3 * D), F32),
        grid=(DEPTH, 3 * D // tn),
        in_specs=[
            pl.BlockSpec((8, D), lambda l, j: (0, 0)),
            pl.BlockSpec((1, D, tn), lambda l, j: (l, 0, j)),
            pl.BlockSpec((1, 1, tn), lambda l, j: (l, 0, j)),
        ],
        out_specs=pl.BlockSpec((1, 8, tn), lambda l, j: (l, 0, j)),
        compiler_params=_params("arbitrary", "arbitrary"),
        name="modulation",
    )(c8, w_mod, b_mod)


def _prenorm_kernel(x_ref, mod_ref, g_ref, h_ref, *, tm):
    m = mod_ref[0]
    g = g_ref[0]
    base = pl.program_id(0) * tm
    rc = 16

    def body(r, carry):
        r0 = pl.multiple_of(r * rc, rc)
        x = x_ref[pl.ds(r0, rc), :]
        ms = jnp.mean(x * x, axis=-1, keepdims=True)
        y = x * lax.rsqrt(ms + EPS) * g
        is_lat = (base + r0) < SEQ
        shift = jnp.where(is_lat, m[0:1, 0:D], m[1:2, 0:D])
        scale = jnp.where(is_lat, m[0:1, D:2 * D], m[1:2, D:2 * D])
        h_ref[pl.ds(r0, rc), :] = (y * (1.0 + scale) + shift).astype(BF16)
        return carry

    lax.fori_loop(0, tm // rc, body, 0, unroll=11)


def _prenorm(x_all, mod, g_pre, l):
    tm = TM_S
    return pl.pallas_call(
        functools.partial(_prenorm_kernel, tm=tm),
        out_shape=jax.ShapeDtypeStruct((S_ALL, D), BF16),
        grid=(S_ALL // tm,),
        in_specs=[
            pl.BlockSpec((tm, D), lambda i: (i, 0)),
            pl.BlockSpec((1, 8, 3 * D), lambda i: (l, 0, 0)),
            pl.BlockSpec((1, 1, D), lambda i: (l, 0, 0)),
        ],
        out_specs=pl.BlockSpec((tm, D), lambda i: (i, 0)),
        compiler_params=_params("arbitrary"),
        name="prenorm",
    )(x_all, mod, g_pre)


def _swap_halves(t):
    lane = lax.broadcasted_iota(jnp.int32, t.shape, 1)
    return jnp.where((lane % 64) < 32, pltpu.roll(t, 96, 1), pltpu.roll(t, 32, 1))


def _inproj_qk_kernel(h_ref, w_ref, nw_ref, cos_ref, sin_ref, o_ref, acc_a, acc_b, *, tn):
    i = pl.program_id(0)
    j = pl.program_id(1)

    @pl.when((i == 0) & (j == 0))
    def _():
        acc_b[...] = jnp.zeros(acc_b.shape, F32)

    def stage(acc_new, acc_old):
        acc_new[...] = jnp.dot(h_ref[...], w_ref[0], preferred_element_type=F32)
        cos = cos_ref[...]
        sin = sin_ref[...]
        for hh in range(tn // HEAD_DIM):
            sl = slice(hh * HEAD_DIM, (hh + 1) * HEAD_DIM)
            t = acc_old[:, sl]
            ms = jnp.mean(t * t, axis=-1, keepdims=True)
            y = t * lax.rsqrt(ms + EPS) * nw_ref[:, sl]
            o_ref[:, sl] = (y * cos + _swap_halves(y) * sin).astype(BF16)

    @pl.when(j % 2 == 0)
    def _():
        stage(acc_a, acc_b)

    @pl.when(j % 2 == 1)
    def _():
        stage(acc_b, acc_a)


def _inproj_qk(h, w_qk, nw, cos_tab, sin_tab, l):
    tn = 512
    nt = (D + KV_DIM) // tn
    return pl.pallas_call(
        functools.partial(_inproj_qk_kernel, tn=tn),
        out_shape=jax.ShapeDtypeStruct((S_ALL, D + KV_DIM), BF16),
        grid=(S_ALL // TM, nt + 1),
        in_specs=[
            pl.BlockSpec((TM, D), lambda i, j: (i, 0)),
            pl.BlockSpec((1, D, tn), lambda i, j: (l, 0, jnp.minimum(j, nt - 1))),
            pl.BlockSpec((1, tn), lambda i, j: (0, jnp.maximum(j - 1, 0))),
            pl.BlockSpec((TM, HEAD_DIM), lambda i, j: (i, 0)),
            pl.BlockSpec((TM, HEAD_DIM), lambda i, j: (i, 0)),
        ],
        out_specs=pl.BlockSpec((TM, tn), lambda i, j: (i, jnp.maximum(j - 1, 0))),
        scratch_shapes=[pltpu.VMEM((TM, tn), F32), pltpu.VMEM((TM, tn), F32)],
        compiler_params=_params("arbitrary", "arbitrary"),
        name="inproj_qk",
    )(h, w_qk, nw, cos_tab, sin_tab)


def _inproj_plain_kernel(h_ref, w_ref, o_ref):
    o_ref[...] = jnp.dot(h_ref[...], w_ref[0].astype(BF16),
                         preferred_element_type=F32).astype(o_ref.dtype)


def _inproj_plain(h, w_in, l, col0, ncols, out_dtype, name):
    tn = 512
    off = col0 // tn
    return pl.pallas_call(
        _inproj_plain_kernel,
        out_shape=jax.ShapeDtypeStruct((S_ALL, ncols), out_dtype),
        grid=(S_ALL // TM_W, ncols // tn),
        in_specs=[
            pl.BlockSpec((TM_W, D), lambda i, j: (i, 0)),
            pl.BlockSpec((1, D, tn), lambda i, j: (l, 0, off + j)),
        ],
        out_specs=pl.BlockSpec((TM_W, tn), lambda i, j: (i, j)),
        compiler_params=_params("arbitrary", "arbitrary"),
        name=name,
    )(h, w_in)


def _attn_kernel(bound_ref, q_ref, k_ref, v_ref, o_ref, q4_ref, m_ref, l_ref, acc_ref,
                 *, tq, tk, groups, bounded):
    qi = pl.program_id(1)
    for hh in range(Q_PER_KV):
        q4_ref[hh * tq:(hh + 1) * tq, :] = q_ref[:, hh * HEAD_DIM:(hh + 1) * HEAD_DIM]
    if not bounded:
        m_ref[...] = jnp.full(m_ref.shape, -jnp.inf, F32)
    l_ref[...] = jnp.zeros(l_ref.shape, F32)
    acc_ref[...] = jnp.zeros(acc_ref.shape, F32)
    rows = Q_PER_KV * tq // groups

    def step(start, size):
        k = k_ref[pl.ds(start, size), :]
        v = v_ref[pl.ds(start, size), :]
        rep = size // HEAD_DIM
        scores = [lax.dot_general(q4_ref[r * rows:(r + 1) * rows, :], k, (((1,), (1,)), ((), ())),
                                  preferred_element_type=F32) for r in range(groups)]
        for r in range(groups):
            rs = slice(r * rows, (r + 1) * rows)
            s = scores[r]
            if bounded:
                p = jnp.exp2(s - bound_ref[0])
                lp = p[:, 0:HEAD_DIM]
                for t in range(1, rep):
                    lp = lp + p[:, t * HEAD_DIM:(t + 1) * HEAD_DIM]
                l_ref[rs, :] += lp
                acc_ref[rs, :] += jnp.dot(p.astype(BF16), v, preferred_element_type=F32)
            else:
                m_prev = m_ref[rs, :]
                m_new = jnp.maximum(m_prev, jnp.max(s, axis=-1, keepdims=True))
                alpha = jnp.exp2(m_prev - m_new)
                p = jnp.exp2(s - _lane_tile(m_new, rep))
                l_ref[rs, :] = alpha * l_ref[rs, :] + jnp.sum(p, axis=-1, keepdims=True)
                acc_ref[rs, :] = alpha * acc_ref[rs, :] + jnp.dot(p.astype(BF16), v,
                                                                  preferred_element_type=F32)
                m_ref[rs, :] = m_new

    @pl.when(qi < SEQ // tq)
    def _():
        def body(c, carry):
            step(pl.multiple_of(c * tk, tk), tk)
            return carry
        lax.fori_loop(0, SEQ // tk, body, 0)

    step(SEQ, CTX)
    l = jnp.sum(l_ref[...], axis=-1, keepdims=True) if bounded else l_ref[...]
    o = acc_ref[...] / l
    for hh in range(Q_PER_KV):
        o_ref[:, hh * HEAD_DIM:(hh + 1) * HEAD_DIM] = o[hh * tq:(hh + 1) * tq, :].astype(o_ref.dtype)


def _attention(qk, v, bound, bounded):
    tq, tk, groups = 256, 4096, 4
    gw = Q_PER_KV * HEAD_DIM
    k_blk = D // HEAD_DIM
    return pl.pallas_call(
        functools.partial(_attn_kernel, tq=tq, tk=tk, groups=groups, bounded=bounded),
        out_shape=jax.ShapeDtypeStruct((S_ALL, D), BF16),
        grid=(N_KV, S_ALL // tq),
        in_specs=[
            pl.BlockSpec(memory_space=pltpu.SMEM),
            pl.BlockSpec((tq, gw), lambda g, qi: (qi, g)),
            pl.BlockSpec((S_ALL, HEAD_DIM), lambda g, qi: (0, k_blk + g)),
            pl.BlockSpec((S_ALL, HEAD_DIM), lambda g, qi: (0, g)),
        ],
        out_specs=pl.BlockSpec((tq, gw), lambda g, qi: (qi, g)),
        scratch_shapes=[
            pltpu.VMEM((Q_PER_KV * tq, HEAD_DIM), BF16),
            pltpu.VMEM((Q_PER_KV * tq, HEAD_DIM), F32),
            pltpu.VMEM((Q_PER_KV * tq, HEAD_DIM), F32),
            pltpu.VMEM((Q_PER_KV * tq, HEAD_DIM), F32),
        ],
        compiler_params=_params("arbitrary", "arbitrary"),
        name="attention_bounded" if bounded else "attention_online",
    )(bound, qk, qk, v)


MAX_SCORE_BOUND = 60.0


def _attention_dispatch(qk, v, qn, kn):
    bound = (1.02 * HEAD_DIM * QK_PRESCALE) * jnp.max(jnp.abs(qn)) * jnp.max(jnp.abs(kn))
    b1 = bound.reshape(1).astype(F32)
    return lax.cond(bound <= MAX_SCORE_BOUND,
                    lambda: _attention(qk, v, b1, True),
                    lambda: _attention(qk, v, b1, False))


FFT_R = 8
FFT_M = SEQ // FFT_R


def _four_lat_a_kernel(u_ref, w_ref, zr_ref, zi_ref, r_ref, *, tm):
    r = jnp.dot(u_ref[...].astype(BF16), w_ref[...], preferred_element_type=F32)
    nt = F_GROUP_DIM // 128
    for t in range(2 * nt):
        r_ref[t] = r[:, t * 128:(t + 1) * 128]
    for n1 in range(FFT_R):
        for t in range(2 * nt):
            blk = r_ref[t, pl.ds(n1, tm // FFT_R, stride=FFT_R), :]
            dst = zr_ref if t < nt else zi_ref
            tt = t % nt
            dst[n1, :, tt * 128:(tt + 1) * 128] = blk.astype(BF16)


def _fourier_lat_a(rest, w_c):
    tm = 1024
    off = OFF_FX // F_GROUP_DIM
    out = jax.ShapeDtypeStruct((FFT_R, FFT_M, F_DIM), BF16)
    zspec = pl.BlockSpec((FFT_R, tm // FFT_R, F_GROUP_DIM), lambda i, g: (0, i, g))
    return pl.pallas_call(
        functools.partial(_four_lat_a_kernel, tm=tm),
        out_shape=(out, out),
        grid=(SEQ // tm, F_GROUPS),
        in_specs=[
            pl.BlockSpec((tm, F_GROUP_DIM), lambda i, g: (i, off + g)),
            pl.BlockSpec((F_GROUP_DIM, 2 * F_GROUP_DIM), lambda i, g: (0, 0)),
        ],
        out_specs=(zspec, zspec),
        scratch_shapes=[pltpu.VMEM((2 * F_GROUP_DIM // 128, tm, 128), F32)],
        compiler_params=_params("arbitrary", "arbitrary"),
        name="fourier_lat_a",
    )(rest, w_c)


def _four_lat_b_kernel(w8_ref, c_ref, s_ref, zr_ref, zi_ref, twc_ref, tws_ref, o_ref, acc_ref, *, tc):
    n1 = pl.program_id(1)

    @pl.when(n1 == 0)
    def _():
        acc_ref[...] = jnp.zeros(acc_ref.shape, F32)

    zr = zr_ref[0]
    zi = zi_ref[0]
    cm = c_ref[...]
    sm = s_ref[...]
    vr = (jnp.dot(cm, zr, preferred_element_type=F32) + jnp.dot(sm, zi, preferred_element_type=F32))
    vi = (jnp.dot(cm, zi, preferred_element_type=F32) - jnp.dot(sm, zr, preferred_element_type=F32))
    twc = _lane_tile(twc_ref[0], tc // 128)
    tws = _lane_tile(tws_ref[0], tc // 128)
    pr = vr * twc + vi * tws
    pi = vi * twc - vr * tws
    for k1 in range(FFT_R):
        acc_ref[k1] += w8_ref[0, n1, k1] * pr + w8_ref[1, n1, k1] * pi

    @pl.when(n1 == FFT_R - 1)
    def _():
        scale = 1.0 / float(SEQ * F_GROUP_DIM) ** 0.5
        for k1 in range(FFT_R):
            o_ref[k1 * FFT_M:(k1 + 1) * FFT_M, :] = (acc_ref[k1] * scale).astype(BF16)
        o_ref[SEQ:, :] = jnp.zeros((CTX, tc), BF16)


def _fourier_lat_b(w8, c_m, s_m, zr, zi, twc, tws):
    tc = 256
    zspec = pl.BlockSpec((1, FFT_M, tc), lambda j, n1: (n1, 0, j))
    twspec = pl.BlockSpec((1, FFT_M, 128), lambda j, n1: (n1, 0, 0))
    mspec = pl.BlockSpec((FFT_M, FFT_M), lambda j, n1: (0, 0))
    return pl.pallas_call(
        functools.partial(_four_lat_b_kernel, tc=tc),
        out_shape=jax.ShapeDtypeStruct((S_ALL, F_DIM), BF16),
        grid=(F_DIM // tc, FFT_R),
        in_specs=[pl.BlockSpec(memory_space=pltpu.SMEM), mspec, mspec, zspec, zspec, twspec, twspec],
        out_specs=pl.BlockSpec((S_ALL, tc), lambda j, n1: (0, j)),
        scratch_shapes=[pltpu.VMEM((FFT_R, FFT_M, tc), F32)],
        compiler_params=_params("arbitrary", "arbitrary"),
        name="fourier_lat_b",
    )(w8, c_m, s_m, zr, zi, twc, tws)


def _four_ctx_kernel(u_ref, w_ref, c_ref, s_ref, lat_ref, o_ref):
    del lat_ref
    z = jnp.dot(u_ref[...].astype(BF16), w_ref[...], preferred_element_type=F32)
    zr = z[:, :F_GROUP_DIM].astype(BF16)
    zi = z[:, F_GROUP_DIM:].astype(BF16)
    y = (jnp.dot(c_ref[...], zr, preferred_element_type=F32)
         + jnp.dot(s_ref[...], zi, preferred_element_type=F32))
    o_ref[...] = (y * (1.0 / float(CTX * F_GROUP_DIM) ** 0.5)).astype(BF16)


def _fourier_ctx(rest, w_c, c_x, s_x, fmix_lat):
    off = OFF_FX // F_GROUP_DIM
    mspec = pl.BlockSpec((CTX, CTX), lambda g: (0, 0))
    return pl.pallas_call(
        _four_ctx_kernel,
        out_shape=jax.ShapeDtypeStruct((S_ALL, F_DIM), BF16),
        grid=(F_GROUPS,),
        in_specs=[
            pl.BlockSpec((CTX, F_GROUP_DIM), lambda g: (SEQ // CTX, off + g)),
            pl.BlockSpec((F_GROUP_DIM, 2 * F_GROUP_DIM), lambda g: (0, 0)),
            mspec, mspec,
            pl.BlockSpec(memory_space=pl.ANY),
        ],
        out_specs=pl.BlockSpec((CTX, F_GROUP_DIM), lambda g: (SEQ // CTX, g)),
        input_output_aliases={4: 0},
        compiler_params=_params("arbitrary"),
        name="fourier_ctx",
    )(rest, w_c, c_x, s_x, fmix_lat)


def _proj_a_kernel(attn_ref, ag_ref, w_ref, o_ref, a_ref):
    @pl.when(pl.program_id(1) == 0)
    def _():
        cw = 512
        for c in range(D // cw):
            sl = slice(c * cw, (c + 1) * cw)
            a_ref[:, sl] = (attn_ref[:, sl].astype(F32) * _silu(ag_ref[:, sl].astype(F32))).astype(BF16)

    o_ref[...] = jnp.dot(a_ref[...], w_ref[0], preferred_element_type=F32)


def _proj_a(attn, rest, w_attn_o, l):
    tm, tn = TM, 1024
    return pl.pallas_call(
        _proj_a_kernel,
        out_shape=jax.ShapeDtypeStruct((S_ALL, D), F32),
        grid=(S_ALL // tm, D // tn),
        in_specs=[
            pl.BlockSpec((tm, D), lambda i, j: (i, 0)),
            pl.BlockSpec((tm, D), lambda i, j: (i, OFF_AG // D)),
            pl.BlockSpec((1, D, tn), lambda i, j: (l, 0, j)),
        ],
        out_specs=pl.BlockSpec((tm, tn), lambda i, j: (i, j)),
        scratch_shapes=[pltpu.VMEM((tm, D), BF16)],
        compiler_params=_params("arbitrary", "arbitrary"),
        name="proj_attn",
    )(attn, rest, w_attn_o)


def _proj_f_kernel(y_ref, fg_ref, wmix_ref, wo_ref, o_ref):
    t = jnp.dot(y_ref[...], wmix_ref[0], preferred_element_type=F32) * _silu(fg_ref[...].astype(F32))
    o_ref[...] = jnp.dot(t.astype(BF16), wo_ref[0], preferred_element_type=F32)


def _proj_f(fmix, rest, w_f_mix, w_f_o, l):
    tm = TM_S
    return pl.pallas_call(
        _proj_f_kernel,
        out_shape=jax.ShapeDtypeStruct((S_ALL, D), F32),
        grid=(S_ALL // tm,),
        in_specs=[
            pl.BlockSpec((tm, F_DIM), lambda i: (i, 0)),
            pl.BlockSpec((tm, F_DIM), lambda i: (i, OFF_FG // F_DIM)),
            pl.BlockSpec((1, F_DIM, F_DIM), lambda i: (l, 0, 0)),
            pl.BlockSpec((1, F_DIM, D), lambda i: (l, 0, 0)),
        ],
        out_specs=pl.BlockSpec((tm, D), lambda i: (i, 0)),
        compiler_params=_params("arbitrary"),
        name="proj_fourier",
    )(fmix, rest, w_f_mix, w_f_o)


def _proj_c_kernel(cx_ref, cb_ref, cc_ref, cg_ref, cxp_ref, ccp_ref, cxn_ref, ccn_ref,
                   cw_ref, cbias_ref, w_ref, o_ref, *, tm):
    hb = HALO_ROWS
    u = cc_ref[...].astype(F32) * cx_ref[...].astype(F32)
    u_before = ccp_ref[hb - 1:hb, :].astype(F32) * cxp_ref[hb - 1:hb, :].astype(F32)
    u_after = ccn_ref[0:1, :].astype(F32) * cxn_ref[0:1, :].astype(F32)
    rows = _row_ids(pl.program_id(0), tm)
    loc = lax.broadcasted_iota(jnp.int32, (tm, 1), 0)
    u_prev = jnp.where(loc == 0, u_before, pltpu.roll(u, 1, 0))
    u_prev = jnp.where((rows == 0) | (rows == SEQ), 0.0, u_prev)
    u_next = jnp.where(loc == tm - 1, u_after, pltpu.roll(u, tm - 1, 0))
    u_next = jnp.where((rows == SEQ - 1) | (rows == S_ALL - 1), 0.0, u_next)
    cw = cw_ref[0]
    conv = u_prev * cw[0:1, :] + u * cw[1:2, :] + u_next * cw[2:3, :] + cbias_ref[0]
    t = cb_ref[...].astype(F32) * conv * _silu(cg_ref[...].astype(F32))
    o_ref[...] = jnp.dot(t.astype(BF16), w_ref[0], preferred_element_type=F32)


HALO_ROWS = 16


def _proj_c(rest, conv_w, conv_b, w_conv_o, l):
    tm = TM_S
    rb = tm // HALO_ROWS
    last = S_ALL // HALO_ROWS - 1

    def col(off):
        return pl.BlockSpec((tm, CONV_DIM), lambda i: (i, off // CONV_DIM))

    def above(off):
        return pl.BlockSpec((HALO_ROWS, CONV_DIM),
                            lambda i: (jnp.maximum(i * rb - 1, 0), off // CONV_DIM))

    def below(off):
        return pl.BlockSpec((HALO_ROWS, CONV_DIM),
                            lambda i: (jnp.minimum((i + 1) * rb, last), off // CONV_DIM))

    return pl.pallas_call(
        functools.partial(_proj_c_kernel, tm=tm),
        out_shape=jax.ShapeDtypeStruct((S_ALL, D), F32),
        grid=(S_ALL // tm,),
        in_specs=[
            col(OFF_CX), col(OFF_CB), col(OFF_CC), col(OFF_CG),
            above(OFF_CX), above(OFF_CC), below(OFF_CX), below(OFF_CC),
            pl.BlockSpec((1, 3, CONV_DIM), lambda i: (l, 0, 0)),
            pl.BlockSpec((1, 1, CONV_DIM), lambda i: (l, 0, 0)),
            pl.BlockSpec((1, CONV_DIM, D), lambda i: (l, 0, 0)),
        ],
        out_specs=pl.BlockSpec((tm, D), lambda i: (i, 0)),
        compiler_params=_params("arbitrary"),
        name="proj_conv",
    )(rest, rest, rest, rest, rest, rest, rest, rest, conv_w, conv_b, w_conv_o)


def _final_kernel(ml0_ref, ml1_ref, ml2_ref, ya_ref, yf_ref, yc_ref, w_ref, x_ref,
                  mod_ref, gp_ref, o_ref, acc_ref, *, tm, nj):
    j = pl.program_id(1)

    @pl.when(j == 0)
    def _():
        acc_ref[...] = jnp.zeros(acc_ref.shape, F32)

    m = (_sigmoid(ml0_ref[...].astype(F32)) * ya_ref[...] + _sigmoid(ml1_ref[...].astype(F32)) * yf_ref[...]
         + _sigmoid(ml2_ref[...].astype(F32)) * yc_ref[...])
    acc_ref[...] += jnp.dot(m.astype(BF16), w_ref[0], preferred_element_type=F32)

    @pl.when(j == nj - 1)
    def _():
        md = mod_ref[0]
        gp = gp_ref[0]
        base = pl.program_id(0) * tm
        rc = 16

        def body(r, carry):
            r0 = pl.multiple_of(r * rc, rc)
            out = acc_ref[pl.ds(r0, rc), :]
            ms = jnp.mean(out * out, axis=-1, keepdims=True)
            nrm = out * lax.rsqrt(ms + EPS) * gp
            is_lat = (base + r0) < SEQ
            gate = jnp.where(is_lat, md[0:1, 2 * D:3 * D], md[1:2, 2 * D:3 * D])
            o_ref[pl.ds(r0, rc), :] = x_ref[pl.ds(r0, rc), :] + gate * nrm
            return carry

        lax.fori_loop(0, tm // rc, body, 0, unroll=11)


def _final(rest, ya, yf, yc, w_out, x_all, mod, g_post, l):
    tm, tn = TM_S, 1024
    nj = D // tn

    def ml(b):
        base = (OFF_ML + b * D) // tn
        return pl.BlockSpec((tm, tn), lambda i, j: (i, base + j))

    ysp = pl.BlockSpec((tm, tn), lambda i, j: (i, j))
    return pl.pallas_call(
        functools.partial(_final_kernel, tm=tm, nj=nj),
        out_shape=jax.ShapeDtypeStruct((S_ALL, D), F32),
        grid=(S_ALL // tm, nj),
        in_specs=[
            ml(0), ml(1), ml(2), ysp, ysp, ysp,
            pl.BlockSpec((1, tn, D), lambda i, j: (l, j, 0)),
            pl.BlockSpec((tm, D), lambda i, j: (i, 0)),
            pl.BlockSpec((1, 8, 3 * D), lambda i, j: (l, 0, 0)),
            pl.BlockSpec((1, 1, D), lambda i, j: (l, 0, 0)),
        ],
        out_specs=pl.BlockSpec((tm, D), lambda i, j: (i, 0)),
        scratch_shapes=[pltpu.VMEM((tm, D), F32)],
        compiler_params=_params("arbitrary", "arbitrary"),
        name="merge_out",
    )(rest, rest, rest, ya, yf, yc, w_out, x_all, mod, g_post)


def _rope_tables():
    half = HEAD_DIM // 4
    pos = jnp.arange(SEQ, dtype=jnp.int32)
    freqs = ROPE_THETA ** (-jnp.arange(half, dtype=F32) / half)
    ang_r = (pos // GRID_W).astype(F32)[:, None] * freqs[None, :]
    ang_c = (pos % GRID_W).astype(F32)[:, None] * freqs[None, :]
    cos = jnp.concatenate([jnp.cos(ang_r), jnp.cos(ang_r), jnp.cos(ang_c), jnp.cos(ang_c)], axis=-1)
    sin = jnp.concatenate([-jnp.sin(ang_r), jnp.sin(ang_r), -jnp.sin(ang_c), jnp.sin(ang_c)], axis=-1)
    cos = jnp.concatenate([cos, jnp.ones((CTX, HEAD_DIM), F32)], axis=0)
    sin = jnp.concatenate([sin, jnp.zeros((CTX, HEAD_DIM), F32)], axis=0)
    return cos, sin


def _dft_cos_sin(n):
    idx = jnp.arange(n, dtype=jnp.int32)
    ang = ((idx[:, None] * idx[None, :]) % n).astype(F32) * (2.0 * jnp.pi / n)
    return jnp.cos(ang), jnp.sin(ang)


def _dft_tables():
    c_c, s_c = _dft_cos_sin(F_GROUP_DIM)
    w_c = jnp.concatenate([c_c, -s_c], axis=1).astype(BF16)
    c_m, s_m = _dft_cos_sin(FFT_M)
    c_x, s_x = _dft_cos_sin(CTX)
    n1 = jnp.arange(FFT_R, dtype=jnp.int32)
    ang = ((n1[:, None] * jnp.arange(FFT_M, dtype=jnp.int32)[None, :]) % SEQ).astype(F32) * (2.0 * jnp.pi / SEQ)
    twc = jnp.broadcast_to(jnp.cos(ang)[:, :, None], (FFT_R, FFT_M, 128))
    tws = jnp.broadcast_to(jnp.sin(ang)[:, :, None], (FFT_R, FFT_M, 128))
    c_8, s_8 = _dft_cos_sin(FFT_R)
    w8 = jnp.stack([c_8, s_8])
    return dict(w_c=w_c, c_m=c_m.astype(BF16), s_m=s_m.astype(BF16), c_x=c_x.astype(BF16),
                s_x=s_x.astype(BF16), twc=twc, tws=tws, w8=w8)


def kernel(x, c, ctx, c_ctx, w_mod, b_mod, g_pre, g_post, w_in, q_norm, k_norm,
           w_attn_o, w_f_mix, w_f_o, conv_w, conv_b, w_conv_o, w_out):
    assert x.shape == (1, SEQ, D) and ctx.shape == (1, CTX, D) and w_in.shape == (DEPTH, D, IN_COLS)
    x_all = jnp.concatenate([x[0], ctx[0]], axis=0)
    c8 = jnp.concatenate([c, c_ctx[None, :], jnp.zeros((6, D), F32)], axis=0)
    cos_tab, sin_tab = _rope_tables()
    ft = _dft_tables()

    w_qk_b = w_in[:, :, :D + KV_DIM].astype(BF16)
    w_attn_o_b = w_attn_o.astype(BF16)
    w_f_mix_b = w_f_mix.astype(BF16)
    w_f_o_b = w_f_o.astype(BF16)
    w_conv_o_b = w_conv_o.astype(BF16)
    w_out_b = w_out.astype(BF16)

    mod = _modulation(c8, w_mod, b_mod.reshape(DEPTH, 1, 3 * D))
    g_pre3 = g_pre.reshape(DEPTH, 1, D)
    g_post3 = g_post.reshape(DEPTH, 1, D)
    conv_b3 = conv_b.reshape(DEPTH, 1, CONV_DIM)

    for l in range(DEPTH):
        nw = jnp.concatenate([jnp.tile(q_norm[l] * QK_PRESCALE, N_HEADS),
                              jnp.tile(k_norm[l], N_KV)])[None, :]
        h = _prenorm(x_all, mod, g_pre3, l)
        qk = _inproj_qk(h, w_qk_b, nw, cos_tab, sin_tab, l)
        v = _inproj_plain(h, w_in, l, D + KV_DIM, KV_DIM, BF16, "inproj_v")
        rest = _inproj_plain(h, w_in, l, QKV_COLS, REST_COLS, BF16, "inproj_rest")
        attn = _attention_dispatch(qk, v, q_norm[l], k_norm[l])
        zr, zi = _fourier_lat_a(rest, ft["w_c"])
        fmix_lat = _fourier_lat_b(ft["w8"], ft["c_m"], ft["s_m"], zr, zi, ft["twc"], ft["tws"])
        fmix = _fourier_ctx(rest, ft["w_c"], ft["c_x"], ft["s_x"], fmix_lat)
        ya = _proj_a(attn, rest, w_attn_o_b, l)
        yf = _proj_f(fmix, rest, w_f_mix_b, w_f_o_b, l)
        yc = _proj_c(rest, conv_w, conv_b3, w_conv_o_b, l)
        x_all = _final(rest, ya, yf, yc, w_out_b, x_all, mod, g_post3, l)

    return x_all[:SEQ][None]
```

```python
import functools

import jax
import jax.numpy as jnp
from jax import lax
from jax.experimental import pallas as pl
from jax.experimental.pallas import tpu as pltpu

D = 2048
SEQ = 8192
CTX = 256
S_ALL = SEQ + CTX
DEPTH = 4
GRID_W = 64
HEAD_DIM = 128
N_HEADS = 16
N_KV = 4
Q_PER_KV = N_HEADS // N_KV
KV_DIM = N_KV * HEAD_DIM
F_GROUPS = 4
F_DIM = 1024
F_GROUP_DIM = F_DIM // F_GROUPS
CONV_DIM = 1024
ROPE_THETA = 10000.0
EPS = 1e-6
QK_PRESCALE = 1.4426950408889634 / float(HEAD_DIM) ** 0.5
IN_COLS = 17408
QKV_COLS = D + 2 * KV_DIM
REST_COLS = IN_COLS - QKV_COLS
OFF_AG, OFF_FX, OFF_FG, OFF_CX, OFF_CB, OFF_CC, OFF_CG, OFF_ML = (
    0, 2048, 3072, 4096, 5120, 6144, 7168, 8192)

TM = 1056
TM_W = 2112
TM_S = 528
VMEM_LIMIT_BYTES = 56 * 1024 * 1024

F32 = jnp.float32
BF16 = jnp.bfloat16


def _params(*sem):
    return pltpu.CompilerParams(dimension_semantics=sem, vmem_limit_bytes=VMEM_LIMIT_BYTES)


def _sigmoid(v):
    return 0.5 * jnp.tanh(0.5 * v) + 0.5


def _silu(v):
    return v * _sigmoid(v)


def _lane_tile(v, rep):
    return v if rep == 1 else jnp.concatenate([v] * rep, axis=1)


def _row_ids(i, tm):
    return i * tm + lax.broadcasted_iota(jnp.int32, (tm, 1), 0)


def _mod_kernel(c_ref, w_ref, b_ref, o_ref):
    sc = _silu(c_ref[...])
    o_ref[0] = jnp.dot(sc.astype(BF16), w_ref[0].astype(BF16), preferred_element_type=F32) + b_ref[0]


def _modulation(c8, w_mod, b_mod):
    tn = 1024
    return pl.pallas_call(
        _mod_kernel,
        out_shape=jax.ShapeDtypeStruct((DEPTH, 8, 3 * D), F32),
        grid=(DEPTH, 3 * D // tn),
        in_specs=[
            pl.BlockSpec((8, D), lambda l, j: (0, 0)),
            pl.BlockSpec((1, D, tn), lambda l, j: (l, 0, j)),
            pl.BlockSpec((1, 1, tn), lambda l, j: (l, 0, j)),
        ],
        out_specs=pl.BlockSpec((1, 8, tn), lambda l, j: (l, 0, j)),
        compiler_params=_params("arbitrary", "arbitrary"),
        name="modulation",
    )(c8, w_mod, b_mod)


def _prenorm_kernel(x_ref, mod_ref, g_ref, h_ref, *, tm):
    m = mod_ref[0]
    g = g_ref[0]
    base = pl.program_id(0) * tm
    rc = 16

    def body(r, carry):
        r0 = pl.multiple_of(r * rc, rc)
        x = x_ref[pl.ds(r0, rc), :]
        ms = jnp.mean(x * x, axis=-1, keepdims=True)
        y = x * lax.rsqrt(ms + EPS) * g
        is_lat = (base + r0) < SEQ
        shift = jnp.where(is_lat, m[0:1, 0:D], m[1:2, 0:D])
        scale = jnp.where(is_lat, m[0:1, D:2 * D], m[1:2, D:2 * D])
        h_ref[pl.ds(r0, rc), :] = (y * (1.0 + scale) + shift).astype(BF16)
        return carry

    lax.fori_loop(0, tm // rc, body, 0, unroll=11)


def _prenorm(x_all, mod, g_pre, l):
    tm = TM_S
    return pl.pallas_call(
        functools.partial(_prenorm_kernel, tm=tm),
        out_shape=jax.ShapeDtypeStruct((S_ALL, D), BF16),
        grid=(S_ALL // tm,),
        in_specs=[
            pl.BlockSpec((tm, D), lambda i: (i, 0)),
            pl.BlockSpec((1, 8, 3 * D), lambda i: (l, 0, 0)),
            pl.BlockSpec((1, 1, D), lambda i: (l, 0, 0)),
        ],
        out_specs=pl.BlockSpec((tm, D), lambda i: (i, 0)),
        compiler_params=_params("arbitrary"),
        name="prenorm",
    )(x_all, mod, g_pre)


def _swap_halves(t):
    lane = lax.broadcasted_iota(jnp.int32, t.shape, 1)
    return jnp.where((lane % 64) < 32, pltpu.roll(t, 96, 1), pltpu.roll(t, 32, 1))


def _inproj_qk_kernel(h_ref, w_ref, nw_ref, cos_ref, sin_ref, o_ref, acc_a, acc_b, *, tn):
    i = pl.program_id(0)
    j = pl.program_id(1)

    @pl.when((i == 0) & (j == 0))
    def _():
        acc_b[...] = jnp.zeros(acc_b.shape, F32)

    def stage(acc_new, acc_old):
        acc_new[...] = jnp.dot(h_ref[...], w_ref[0], preferred_element_type=F32)
        cos = cos_ref[...]
        sin = sin_ref[...]
        for hh in range(tn // HEAD_DIM):
            sl = slice(hh * HEAD_DIM, (hh + 1) * HEAD_DIM)
            t = acc_old[:, sl]
            ms = jnp.mean(t * t, axis=-1, keepdims=True)
            y = t * lax.rsqrt(ms + EPS) * nw_ref[:, sl]
            o_ref[:, sl] = (y * cos + _swap_halves(y) * sin).astype(BF16)

    @pl.when(j % 2 == 0)
    def _():
        stage(acc_a, acc_b)

    @pl.when(j % 2 == 1)
    def _():
        stage(acc_b, acc_a)


def _inproj_qk(h, w_qk, nw, cos_tab, sin_tab, l):
    tn = 512
    nt = (D + KV_DIM) // tn
    return pl.pallas_call(
        functools.partial(_inproj_qk_kernel, tn=tn),
        out_shape=jax.ShapeDtypeStruct((S_ALL, D + KV_DIM), BF16),
        grid=(S_ALL // TM, nt + 1),
        in_specs=[
            pl.BlockSpec((TM, D), lambda i, j: (i, 0)),
            pl.BlockSpec((1, D, tn), lambda i, j: (l, 0, jnp.minimum(j, nt - 1))),
            pl.BlockSpec((1, tn), lambda i, j: (0, jnp.maximum(j - 1, 0))),
            pl.BlockSpec((TM, HEAD_DIM), lambda i, j: (i, 0)),
            pl.BlockSpec((TM, HEAD_DIM), lambda i, j: (i, 0)),
        ],
        out_specs=pl.BlockSpec((TM, tn), lambda i, j: (i, jnp.maximum(j - 1, 0))),
        scratch_shapes=[pltpu.VMEM((TM, tn), F32), pltpu.VMEM((TM, tn), F32)],
        compiler_params=_params("arbitrary", "arbitrary"),
        name="inproj_qk",
    )(h, w_qk, nw, cos_tab, sin_tab)


def _inproj_plain_kernel(h_ref, w_ref, o_ref):
    o_ref[...] = jnp.dot(h_ref[...], w_ref[0].astype(BF16),
                         preferred_element_type=F32).astype(o_ref.dtype)


def _inproj_plain(h, w_in, l, col0, ncols, out_dtype, name):
    tn = 512
    off = col0 // tn
    return pl.pallas_call(
        _inproj_plain_kernel,
        out_shape=jax.ShapeDtypeStruct((S_ALL, ncols), out_dtype),
        grid=(S_ALL // TM_W, ncols // tn),
        in_specs=[
            pl.BlockSpec((TM_W, D), lambda i, j: (i, 0)),
            pl.BlockSpec((1, D, tn), lambda i, j: (l, 0, off + j)),
        ],
        out_specs=pl.BlockSpec((TM_W, tn), lambda i, j: (i, j)),
        compiler_params=_params("arbitrary", "arbitrary"),
        name=name,
    )(h, w_in)


def _attn_kernel(bound_ref, q_ref, k_ref, v_ref, o_ref, q4_ref, m_ref, l_ref, acc_ref,
                 *, tq, tk, groups, bounded):
    qi = pl.program_id(1)
    for hh in range(Q_PER_KV):
        q4_ref[hh * tq:(hh + 1) * tq, :] = q_ref[:, hh * HEAD_DIM:(hh + 1) * HEAD_DIM]
    if not bounded:
        m_ref[...] = jnp.full(m_ref.shape, -jnp.inf, F32)
    l_ref[...] = jnp.zeros(l_ref.shape, F32)
    acc_ref[...] = jnp.zeros(acc_ref.shape, F32)
    rows = Q_PER_KV * tq // groups

    def step(start, size):
        k = k_ref[pl.ds(start, size), :]
        v = v_ref[pl.ds(start, size), :]
        rep = size // HEAD_DIM
        scores = [lax.dot_general(q4_ref[r * rows:(r + 1) * rows, :], k, (((1,), (1,)), ((), ())),
                                  preferred_element_type=F32) for r in range(groups)]
        for r in range(groups):
            rs = slice(r * rows, (r + 1) * rows)
            s = scores[r]
            if bounded:
                p = jnp.exp2(s - bound_ref[0])
                lp = p[:, 0:HEAD_DIM]
                for t in range(1, rep):
                    lp = lp + p[:, t * HEAD_DIM:(t + 1) * HEAD_DIM]
                l_ref[rs, :] += lp
                acc_ref[rs, :] += jnp.dot(p.astype(BF16), v, preferred_element_type=F32)
            else:
                m_prev = m_ref[rs, :]
                m_new = jnp.maximum(m_prev, jnp.max(s, axis=-1, keepdims=True))
                alpha = jnp.exp2(m_prev - m_new)
                p = jnp.exp2(s - _lane_tile(m_new, rep))
                l_ref[rs, :] = alpha * l_ref[rs, :] + jnp.sum(p, axis=-1, keepdims=True)
                acc_ref[rs, :] = alpha * acc_ref[rs, :] + jnp.dot(p.astype(BF16), v,
                                                                  preferred_element_type=F32)
                m_ref[rs, :] = m_new

    @pl.when(qi < SEQ // tq)
    def _():
        def body(c, carry):
            step(pl.multiple_of(c * tk, tk), tk)
            return carry
        lax.fori_loop(0, SEQ // tk, body, 0)

    step(SEQ, CTX)
    l = jnp.sum(l_ref[...], axis=-1, keepdims=True) if bounded else l_ref[...]
    o = acc_ref[...] / l
    for hh in range(Q_PER_KV):
        o_ref[:, hh * HEAD_DIM:(hh + 1) * HEAD_DIM] = o[hh * tq:(hh + 1) * tq, :].astype(o_ref.dtype)


def _attention(qk, v, bound, bounded):
    tq, tk, groups = 256, 4096, 4
    gw = Q_PER_KV * HEAD_DIM
    k_blk = D // HEAD_DIM
    return pl.pallas_call(
        functools.partial(_attn_kernel, tq=tq, tk=tk, groups=groups, bounded=bounded),
        out_shape=jax.ShapeDtypeStruct((S_ALL, D), BF16),
        grid=(N_KV, S_ALL // tq),
        in_specs=[
            pl.BlockSpec(memory_space=pltpu.SMEM),
            pl.BlockSpec((tq, gw), lambda g, qi: (qi, g)),
            pl.BlockSpec((S_ALL, HEAD_DIM), lambda g, qi: (0, k_blk + g)),
            pl.BlockSpec((S_ALL, HEAD_DIM), lambda g, qi: (0, g)),
        ],
        out_specs=pl.BlockSpec((tq, gw), lambda g, qi: (qi, g)),
        scratch_shapes=[
            pltpu.VMEM((Q_PER_KV * tq, HEAD_DIM), BF16),
            pltpu.VMEM((Q_PER_KV * tq, HEAD_DIM), F32),
            pltpu.VMEM((Q_PER_KV * tq, HEAD_DIM), F32),
            pltpu.VMEM((Q_PER_KV * tq, HEAD_DIM), F32),
        ],
        compiler_params=_params("arbitrary", "arbitrary"),
        name="attention_bounded" if bounded else "attention_online",
    )(bound, qk, qk, v)


MAX_SCORE_BOUND = 60.0


def _attention_dispatch(qk, v, qn, kn):
    bound = (1.02 * HEAD_DIM * QK_PRESCALE) * jnp.max(jnp.abs(qn)) * jnp.max(jnp.abs(kn))
    b1 = bound.reshape(1).astype(F32)
    return lax.cond(bound <= MAX_SCORE_BOUND,
                    lambda: _attention(qk, v, b1, True),
                    lambda: _attention(qk, v, b1, False))


FFT_R = 8
FFT_M = SEQ // FFT_R


def _four_lat_a_kernel(u_ref, w_ref, zr_ref, zi_ref, r_ref, *, tm):
    r = jnp.dot(u_ref[...].astype(BF16), w_ref[...], preferred_element_type=F32)
    nt = F_GROUP_DIM // 128
    for t in range(2 * nt):
        r_ref[t] = r[:, t * 128:(t + 1) * 128]
    for n1 in range(FFT_R):
        for t in range(2 * nt):
            blk = r_ref[t, pl.ds(n1, tm // FFT_R, stride=FFT_R), :]
            dst = zr_ref if t < nt else zi_ref
            tt = t % nt
            dst[n1, :, tt * 128:(tt + 1) * 128] = blk.astype(BF16)


def _fourier_lat_a(rest, w_c):
    tm = 1024
    off = OFF_FX // F_GROUP_DIM
    out = jax.ShapeDtypeStruct((FFT_R, FFT_M, F_DIM), BF16)
    zspec = pl.BlockSpec((FFT_R, tm // FFT_R, F_GROUP_DIM), lambda i, g: (0, i, g))
    return pl.pallas_call(
        functools.partial(_four_lat_a_kernel, tm=tm),
        out_shape=(out, out),
        grid=(SEQ // tm, F_GROUPS),
        in_specs=[
            pl.BlockSpec((tm, F_GROUP_DIM), lambda i, g: (i, off + g)),
            pl.BlockSpec((F_GROUP_DIM, 2 * F_GROUP_DIM), lambda i, g: (0, 0)),
        ],
        out_specs=(zspec, zspec),
        scratch_shapes=[pltpu.VMEM((2 * F_GROUP_DIM // 128, tm, 128), F32)],
        compiler_params=_params("arbitrary", "arbitrary"),
        name="fourier_lat_a",
    )(rest, w_c)


def _four_lat_b_kernel(w8_ref, c_ref, s_ref, zr_ref, zi_ref, twc_ref, tws_ref, o_ref, acc_ref, *, tc):
    n1 = pl.program_id(1)

    @pl.when(n1 == 0)
    def _():
        acc_ref[...] = jnp.zeros(acc_ref.shape, F32)

    zr = zr_ref[0]
    zi = zi_ref[0]
    cm = c_ref[...]
    sm = s_ref[...]
    vr = (jnp.dot(cm, zr, preferred_element_type=F32) + jnp.dot(sm, zi, preferred_element_type=F32))
    vi = (jnp.dot(cm, zi, preferred_element_type=F32) - jnp.dot(sm, zr, preferred_element_type=F32))
    twc = _lane_tile(twc_ref[0], tc // 128)
    tws = _lane_tile(tws_ref[0], tc // 128)
    pr = vr * twc + vi * tws
    pi = vi * twc - vr * tws
    for k1 in range(FFT_R):
        acc_ref[k1] += w8_ref[0, n1, k1] * pr + w8_ref[1, n1, k1] * pi

    @pl.when(n1 == FFT_R - 1)
    def _():
        scale = 1.0 / float(SEQ * F_GROUP_DIM) ** 0.5
        for k1 in range(FFT_R):
            o_ref[k1 * FFT_M:(k1 + 1) * FFT_M, :] = (acc_ref[k1] * scale).astype(BF16)
        o_ref[SEQ:, :] = jnp.zeros((CTX, tc), BF16)


def _fourier_lat_b(w8, c_m, s_m, zr, zi, twc, tws):
    tc = 256
    zspec = pl.BlockSpec((1, FFT_M, tc), lambda j, n1: (n1, 0, j))
    twspec = pl.BlockSpec((1, FFT_M, 128), lambda j, n1: (n1, 0, 0))
    mspec = pl.BlockSpec((FFT_M, FFT_M), lambda j, n1: (0, 0))
    return pl.pallas_call(
        functools.partial(_four_lat_b_kernel, tc=tc),
        out_shape=jax.ShapeDtypeStruct((S_ALL, F_DIM), BF16),
        grid=(F_DIM // tc, FFT_R),
        in_specs=[pl.BlockSpec(memory_space=pltpu.SMEM), mspec, mspec, zspec, zspec, twspec, twspec],
        out_specs=pl.BlockSpec((S_ALL, tc), lambda j, n1: (0, j)),
        scratch_shapes=[pltpu.VMEM((FFT_R, FFT_M, tc), F32)],
        compiler_params=_params("arbitrary", "arbitrary"),
        name="fourier_lat_b",
    )(w8, c_m, s_m, zr, zi, twc, tws)


def _four_ctx_kernel(u_ref, w_ref, c_ref, s_ref, lat_ref, o_ref):
    del lat_ref
    z = jnp.dot(u_ref[...].astype(BF16), w_ref[...], preferred_element_type=F32)
    zr = z[:, :F_GROUP_DIM].astype(BF16)
    zi = z[:, F_GROUP_DIM:].astype(BF16)
    y = (jnp.dot(c_ref[...], zr, preferred_element_type=F32)
         + jnp.dot(s_ref[...], zi, preferred_element_type=F32))
    o_ref[...] = (y * (1.0 / float(CTX * F_GROUP_DIM) ** 0.5)).astype(BF16)


def _fourier_ctx(rest, w_c, c_x, s_x, fmix_lat):
    off = OFF_FX // F_GROUP_DIM
    mspec = pl.BlockSpec((CTX, CTX), lambda g: (0, 0))
    return pl.pallas_call(
        _four_ctx_kernel,
        out_shape=jax.ShapeDtypeStruct((S_ALL, F_DIM), BF16),
        grid=(F_GROUPS,),
        in_specs=[
            pl.BlockSpec((CTX, F_GROUP_DIM), lambda g: (SEQ // CTX, off + g)),
            pl.BlockSpec((F_GROUP_DIM, 2 * F_GROUP_DIM), lambda g: (0, 0)),
            mspec, mspec,
            pl.BlockSpec(memory_space=pl.ANY),
        ],
        out_specs=pl.BlockSpec((CTX, F_GROUP_DIM), lambda g: (SEQ // CTX, g)),
        input_output_aliases={4: 0},
        compiler_params=_params("arbitrary"),
        name="fourier_ctx",
    )(rest, w_c, c_x, s_x, fmix_lat)


def _proj_a_kernel(attn_ref, ag_ref, w_ref, o_ref, a_ref):
    @pl.when(pl.program_id(1) == 0)
    def _():
        cw = 512
        for c in range(D // cw):
            sl = slice(c * cw, (c + 1) * cw)
            a_ref[:, sl] = (attn_ref[:, sl].astype(F32) * _silu(ag_ref[:, sl].astype(F32))).astype(BF16)

    o_ref[...] = jnp.dot(a_ref[...], w_ref[0], preferred_element_type=F32).astype(o_ref.dtype)


def _proj_a(attn, rest, w_attn_o, l):
    tm, tn = TM, 1024
    return pl.pallas_call(
        _proj_a_kernel,
        out_shape=jax.ShapeDtypeStruct((S_ALL, D), BF16),
        grid=(S_ALL // tm, D // tn),
        in_specs=[
            pl.BlockSpec((tm, D), lambda i, j: (i, 0)),
            pl.BlockSpec((tm, D), lambda i, j: (i, OFF_AG // D)),
            pl.BlockSpec((1, D, tn), lambda i, j: (l, 0, j)),
        ],
        out_specs=pl.BlockSpec((tm, tn), lambda i, j: (i, j)),
        scratch_shapes=[pltpu.VMEM((tm, D), BF16)],
        compiler_params=_params("arbitrary", "arbitrary"),
        name="proj_attn",
    )(attn, rest, w_attn_o)


def _proj_f_kernel(y_ref, fg_ref, wmix_ref, wo_ref, o_ref):
    t = jnp.dot(y_ref[...], wmix_ref[0], preferred_element_type=F32) * _silu(fg_ref[...].astype(F32))
    o_ref[...] = jnp.dot(t.astype(BF16), wo_ref[0], preferred_element_type=F32).astype(o_ref.dtype)


def _proj_f(fmix, rest, w_f_mix, w_f_o, l):
    tm = TM_S
    return pl.pallas_call(
        _proj_f_kernel,
        out_shape=jax.ShapeDtypeStruct((S_ALL, D), BF16),
        grid=(S_ALL // tm,),
        in_specs=[
            pl.BlockSpec((tm, F_DIM), lambda i: (i, 0)),
            pl.BlockSpec((tm, F_DIM), lambda i: (i, OFF_FG // F_DIM)),
            pl.BlockSpec((1, F_DIM, F_DIM), lambda i: (l, 0, 0)),
            pl.BlockSpec((1, F_DIM, D), lambda i: (l, 0, 0)),
        ],
        out_specs=pl.BlockSpec((tm, D), lambda i: (i, 0)),
        compiler_params=_params("arbitrary"),
        name="proj_fourier",
    )(fmix, rest, w_f_mix, w_f_o)


def _proj_c_kernel(cx_ref, cb_ref, cc_ref, cg_ref, cxp_ref, ccp_ref, cxn_ref, ccn_ref,
                   cw_ref, cbias_ref, w_ref, o_ref, *, tm):
    hb = HALO_ROWS
    u = cc_ref[...].astype(F32) * cx_ref[...].astype(F32)
    u_before = ccp_ref[hb - 1:hb, :].astype(F32) * cxp_ref[hb - 1:hb, :].astype(F32)
    u_after = ccn_ref[0:1, :].astype(F32) * cxn_ref[0:1, :].astype(F32)
    rows = _row_ids(pl.program_id(0), tm)
    loc = lax.broadcasted_iota(jnp.int32, (tm, 1), 0)
    u_prev = jnp.where(loc == 0, u_before, pltpu.roll(u, 1, 0))
    u_prev = jnp.where((rows == 0) | (rows == SEQ), 0.0, u_prev)
    u_next = jnp.where(loc == tm - 1, u_after, pltpu.roll(u, tm - 1, 0))
    u_next = jnp.where((rows == SEQ - 1) | (rows == S_ALL - 1), 0.0, u_next)
    cw = cw_ref[0]
    conv = u_prev * cw[0:1, :] + u * cw[1:2, :] + u_next * cw[2:3, :] + cbias_ref[0]
    t = cb_ref[...].astype(F32) * conv * _silu(cg_ref[...].astype(F32))
    o_ref[...] = jnp.dot(t.astype(BF16), w_ref[0], preferred_element_type=F32).astype(o_ref.dtype)


HALO_ROWS = 16


def _proj_c(rest, conv_w, conv_b, w_conv_o, l):
    tm = TM_S
    rb = tm // HALO_ROWS
    last = S_ALL // HALO_ROWS - 1

    def col(off):
        return pl.BlockSpec((tm, CONV_DIM), lambda i: (i, off // CONV_DIM))

    def above(off):
        return pl.BlockSpec((HALO_ROWS, CONV_DIM),
                            lambda i: (jnp.maximum(i * rb - 1, 0), off // CONV_DIM))

    def below(off):
        return pl.BlockSpec((HALO_ROWS, CONV_DIM),
                            lambda i: (jnp.minimum((i + 1) * rb, last), off // CONV_DIM))

    return pl.pallas_call(
        functools.partial(_proj_c_kernel, tm=tm),
        out_shape=jax.ShapeDtypeStruct((S_ALL, D), BF16),
        grid=(S_ALL // tm,),
        in_specs=[
            col(OFF_CX), col(OFF_CB), col(OFF_CC), col(OFF_CG),
            above(OFF_CX), above(OFF_CC), below(OFF_CX), below(OFF_CC),
            pl.BlockSpec((1, 3, CONV_DIM), lambda i: (l, 0, 0)),
            pl.BlockSpec((1, 1, CONV_DIM), lambda i: (l, 0, 0)),
            pl.BlockSpec((1, CONV_DIM, D), lambda i: (l, 0, 0)),
        ],
        out_specs=pl.BlockSpec((tm, D), lambda i: (i, 0)),
        compiler_params=_params("arbitrary"),
        name="proj_conv",
    )(rest, rest, rest, rest, rest, rest, rest, rest, conv_w, conv_b, w_conv_o)


def _final_kernel(ml0_ref, ml1_ref, ml2_ref, ya_ref, yf_ref, yc_ref, w_ref, x_ref,
                  mod_ref, gp_ref, o_ref, acc_ref, *, tm, nj):
    j = pl.program_id(1)

    @pl.when(j == 0)
    def _():
        acc_ref[...] = jnp.zeros(acc_ref.shape, F32)

    m = (_sigmoid(ml0_ref[...].astype(F32)) * ya_ref[...].astype(F32)
         + _sigmoid(ml1_ref[...].astype(F32)) * yf_ref[...].astype(F32)
         + _sigmoid(ml2_ref[...].astype(F32)) * yc_ref[...].astype(F32))
    acc_ref[...] += jnp.dot(m.astype(BF16), w_ref[0], preferred_element_type=F32)

    @pl.when(j == nj - 1)
    def _():
        md = mod_ref[0]
        gp = gp_ref[0]
        base = pl.program_id(0) * tm
        rc = 16

        def body(r, carry):
            r0 = pl.multiple_of(r * rc, rc)
            out = acc_ref[pl.ds(r0, rc), :]
            ms = jnp.mean(out * out, axis=-1, keepdims=True)
            nrm = out * lax.rsqrt(ms + EPS) * gp
            is_lat = (base + r0) < SEQ
            gate = jnp.where(is_lat, md[0:1, 2 * D:3 * D], md[1:2, 2 * D:3 * D])
            o_ref[pl.ds(r0, rc), :] = x_ref[pl.ds(r0, rc), :] + gate * nrm
            return carry

        lax.fori_loop(0, tm // rc, body, 0, unroll=11)


def _final(rest, ya, yf, yc, w_out, x_all, mod, g_post, l):
    tm, tn = TM_S, 1024
    nj = D // tn

    def ml(b):
        base = (OFF_ML + b * D) // tn
        return pl.BlockSpec((tm, tn), lambda i, j: (i, base + j))

    ysp = pl.BlockSpec((tm, tn), lambda i, j: (i, j))
    return pl.pallas_call(
        functools.partial(_final_kernel, tm=tm, nj=nj),
        out_shape=jax.ShapeDtypeStruct((S_ALL, D), F32),
        grid=(S_ALL // tm, nj),
        in_specs=[
            ml(0), ml(1), ml(2), ysp, ysp, ysp,
            pl.BlockSpec((1, tn, D), lambda i, j: (l, j, 0)),
            pl.BlockSpec((tm, D), lambda i, j: (i, 0)),
            pl.BlockSpec((1, 8, 3 * D), lambda i, j: (l, 0, 0)),
            pl.BlockSpec((1, 1, D), lambda i, j: (l, 0, 0)),
        ],
        out_specs=pl.BlockSpec((tm, D), lambda i, j: (i, 0)),
        scratch_shapes=[pltpu.VMEM((tm, D), F32)],
        compiler_params=_params("arbitrary", "arbitrary"),
        name="merge_out",
    )(rest, rest, rest, ya, yf, yc, w_out, x_all, mod, g_post)


def _rope_tables():
    half = HEAD_DIM // 4
    pos = jnp.arange(SEQ, dtype=jnp.int32)
    freqs = ROPE_THETA ** (-jnp.arange(half, dtype=F32) / half)
    ang_r = (pos // GRID_W).astype(F32)[:, None] * freqs[None, :]
    ang_c = (pos % GRID_W).astype(F32)[:, None] * freqs[None, :]
    cos = jnp.concatenate([jnp.cos(ang_r), jnp.cos(ang_r), jnp.cos(ang_c), jnp.cos(ang_c)], axis=-1)
    sin = jnp.concatenate([-jnp.sin(ang_r), jnp.sin(ang_r), -jnp.sin(ang_c), jnp.sin(ang_c)], axis=-1)
    cos = jnp.concatenate([cos, jnp.ones((CTX, HEAD_DIM), F32)], axis=0)
    sin = jnp.concatenate([sin, jnp.zeros((CTX, HEAD_DIM), F32)], axis=0)
    return cos, sin


def _dft_cos_sin(n):
    idx = jnp.arange(n, dtype=jnp.int32)
    ang = ((idx[:, None] * idx[None, :]) % n).astype(F32) * (2.0 * jnp.pi / n)
    return jnp.cos(ang), jnp.sin(ang)


def _dft_tables():
    c_c, s_c = _dft_cos_sin(F_GROUP_DIM)
    w_c = jnp.concatenate([c_c, -s_c], axis=1).astype(BF16)
    c_m, s_m = _dft_cos_sin(FFT_M)
    c_x, s_x = _dft_cos_sin(CTX)
    n1 = jnp.arange(FFT_R, dtype=jnp.int32)
    ang = ((n1[:, None] * jnp.arange(FFT_M, dtype=jnp.int32)[None, :]) % SEQ).astype(F32) * (2.0 * jnp.pi / SEQ)
    twc = jnp.broadcast_to(jnp.cos(ang)[:, :, None], (FFT_R, FFT_M, 128))
    tws = jnp.broadcast_to(jnp.sin(ang)[:, :, None], (FFT_R, FFT_M, 128))
    c_8, s_8 = _dft_cos_sin(FFT_R)
    w8 = jnp.stack([c_8, s_8])
    return dict(w_c=w_c, c_m=c_m.astype(BF16), s_m=s_m.astype(BF16), c_x=c_x.astype(BF16),
                s_x=s_x.astype(BF16), twc=twc, tws=tws, w8=w8)


def kernel(x, c, ctx, c_ctx, w_mod, b_mod, g_pre, g_post, w_in, q_norm, k_norm,
           w_attn_o, w_f_mix, w_f_o, conv_w, conv_b, w_conv_o, w_out):
    assert x.shape == (1, SEQ, D) and ctx.shape == (1, CTX, D) and w_in.shape == (DEPTH, D, IN_COLS)
    x_all = jnp.concatenate([x[0], ctx[0]], axis=0)
    c8 = jnp.concatenate([c, c_ctx[None, :], jnp.zeros((6, D), F32)], axis=0)
    cos_tab, sin_tab = _rope_tables()
    ft = _dft_tables()

    w_qk_b = w_in[:, :, :D + KV_DIM].astype(BF16)
    w_attn_o_b = w_attn_o.astype(BF16)
    w_f_mix_b = w_f_mix.astype(BF16)
    w_f_o_b = w_f_o.astype(BF16)
    w_conv_o_b = w_conv_o.astype(BF16)
    w_out_b = w_out.astype(BF16)

    mod = _modulation(c8, w_mod, b_mod.reshape(DEPTH, 1, 3 * D))
    g_pre3 = g_pre.reshape(DEPTH, 1, D)
    g_post3 = g_post.reshape(DEPTH, 1, D)
    conv_b3 = conv_b.reshape(DEPTH, 1, CONV_DIM)

    for l in range(DEPTH):
        nw = jnp.concatenate([jnp.tile(q_norm[l] * QK_PRESCALE, N_HEADS),
                              jnp.tile(k_norm[l], N_KV)])[None, :]
        h = _prenorm(x_all, mod, g_pre3, l)
        qk = _inproj_qk(h, w_qk_b, nw, cos_tab, sin_tab, l)
        v = _inproj_plain(h, w_in, l, D + KV_DIM, KV_DIM, BF16, "inproj_v")
        rest = _inproj_plain(h, w_in, l, QKV_COLS, REST_COLS, BF16, "inproj_rest")
        attn = _attention_dispatch(qk, v, q_norm[l], k_norm[l])
        zr, zi = _fourier_lat_a(rest, ft["w_c"])
        fmix_lat = _fourier_lat_b(ft["w8"], ft["c_m"], ft["s_m"], zr, zi, ft["twc"], ft["tws"])
        fmix = _fourier_ctx(rest, ft["w_c"], ft["c_x"], ft["s_x"], fmix_lat)
        ya = _proj_a(attn, rest, w_attn_o_b, l)
        yf = _proj_f(fmix, rest, w_f_mix_b, w_f_o_b, l)
        yc = _proj_c(rest, conv_w, conv_b3, w_conv_o_b, l)
        x_all = _final(rest, ya, yf, yc, w_out_b, x_all, mod, g_post3, l)

    return x_all[:SEQ][None]
```
